```python
import jax, jax.numpy as jnp
from jax import lax
import numpy as np

D_MODEL = 1024
BATCH = 4
SEQ = 8192
DEPTH = 2

N_MIXERS = 2
D_FF = 2816
EPS = 1e-6

SSD_EXPAND = 2
SSD_D_INNER = SSD_EXPAND * D_MODEL
SSD_HEAD_DIM = 64
SSD_HEADS = SSD_D_INNER // SSD_HEAD_DIM
SSD_GROUPS = 8
SSD_STATE = 128
SSD_CONV = 5
SSD_CHUNK = 128
SSD_CONV_CH = SSD_D_INNER + 2 * SSD_GROUPS * SSD_STATE
SSD_IN = SSD_D_INNER + SSD_CONV_CH + 2 * SSD_HEADS

ATTN_HEADS = 16
ATTN_KV_HEADS = 4
ATTN_HEAD_DIM = 64
ATTN_QKV = (ATTN_HEADS + 2 * ATTN_KV_HEADS) * ATTN_HEAD_DIM
WINDOW = 128
ATTN_BLOCK = 128
ROPE_THETA = 10000.0

N_SSD_LAYERS = (DEPTH + 1) // 2
N_ATTN_LAYERS = DEPTH // 2

kernel_name = "hybrid_ssd_swa_macaron_encoder"


def rms_norm(x, w):
    xf = x.astype(jnp.float32)
    y = xf * lax.rsqrt(jnp.mean(xf * xf, axis=-1, keepdims=True) + EPS)
    return (y * w.astype(jnp.float32)).astype(x.dtype)


def swiglu(h, w_gate, w_up, w_down):
    return (jax.nn.silu(h @ w_gate) * (h @ w_up)) @ w_down


def rotary(x, pos):
    half = x.shape[-1] // 2
    inv_freq = ROPE_THETA ** (-jnp.arange(half, dtype=jnp.float32) / half)
    ang = pos.astype(jnp.float32)[:, None] * inv_freq[None, :]
    cos = jnp.cos(ang)[None, :, None, :]
    sin = jnp.sin(ang)[None, :, None, :]
    xf = x.astype(jnp.float32)
    x1, x2 = xf[..., :half], xf[..., half:]
    return jnp.concatenate([x1 * cos - x2 * sin, x2 * cos + x1 * sin], axis=-1).astype(x.dtype)


def dwconv_centred(u, w, b):
    k = w.shape[0]
    out = lax.conv_general_dilated(
        u, w[:, None, :].astype(u.dtype), window_strides=(1,),
        padding=[((k - 1) // 2, k // 2)],
        dimension_numbers=('NWC', 'WIO', 'NWC'),
        feature_group_count=u.shape[-1])
    return out + b.astype(u.dtype)


def ssd_scan(x, dt, a, b_mat, c_mat):
    f32 = jnp.float32
    bsz, L, H, P = x.shape
    G, N = b_mat.shape[-2:]
    E = H // G
    T = SSD_CHUNK
    nc = L // T
    xdt = (x.astype(f32) * dt[..., None]).reshape(bsz, nc, T, G, E, P)
    bc = b_mat.astype(f32).reshape(bsz, nc, T, G, N)
    cc = c_mat.astype(f32).reshape(bsz, nc, T, G, N)
    da = (dt * a).reshape(bsz, nc, T, G, E)
    cum = jnp.cumsum(da, axis=2).transpose(0, 1, 3, 4, 2)
    lower = jnp.tril(jnp.ones((T, T), dtype=bool))
    seg = cum[..., :, None] - cum[..., None, :]
    decay = jnp.exp(jnp.where(lower, seg, -jnp.inf))
    cb = jnp.einsum('bctgn,bcsgn->bcgts', cc, bc)
    y_diag = jnp.einsum('bcgts,bcgets,bcsgep->bctgep', cb, decay, xdt)
    decay_to_end = jnp.exp(cum[..., -1:] - cum)
    states = jnp.einsum('bctgn,bcget,bctgep->bcgepn', bc, decay_to_end, xdt)
    chunk_decay = jnp.exp(cum[..., -1])

    def step(h, inp):
        s_k, d_k = inp
        return h * d_k[..., None, None] + s_k, h

    h0 = jnp.zeros((bsz, G, E, P, N), f32)
    _, h_in = lax.scan(step, h0, (jnp.moveaxis(states, 1, 0), jnp.moveaxis(chunk_decay, 1, 0)))
    h_in = jnp.moveaxis(h_in, 0, 1)
    y_off = jnp.einsum('bctgn,bcgepn,bcget->bctgep', cc, h_in, jnp.exp(cum))
    return (y_diag + y_off).reshape(bsz, L, H, P)


def ssd_mixer(h, w_in, conv_w, conv_b, dt_bias, a_log, d_skip, norm_w, w_out):
    f32 = jnp.float32
    bsz, L, _ = h.shape
    proj = h @ w_in
    z, xbc, dt_raw = jnp.split(proj, [SSD_D_INNER, SSD_D_INNER + SSD_CONV_CH], axis=-1)
    xbc = jax.nn.silu(dwconv_centred(xbc, conv_w, conv_b))
    xs, b_mat, c_mat = jnp.split(xbc, [SSD_D_INNER, SSD_D_INNER + SSD_GROUPS * SSD_STATE], axis=-1)
    xs = xs.reshape(bsz, L, SSD_HEADS, SSD_HEAD_DIM)
    b_mat = b_mat.reshape(bsz, L, SSD_GROUPS, SSD_STATE)
    c_mat = c_mat.reshape(bsz, L, SSD_GROUPS, SSD_STATE)
    dt = jax.nn.softplus(dt_raw.astype(f32).reshape(bsz, L, 2, SSD_HEADS) + dt_bias.astype(f32))
    a = -jnp.exp(a_log.astype(f32))
    y_fwd = ssd_scan(xs, dt[:, :, 0], a[0], b_mat, c_mat)
    flip = lambda t: jnp.flip(t, axis=1)
    y_bwd = flip(ssd_scan(flip(xs), flip(dt[:, :, 1]), a[1], flip(b_mat), flip(c_mat)))
    y = y_fwd + y_bwd + xs.astype(f32) * d_skip.astype(f32)[:, None]
    y = y.reshape(bsz, L, SSD_D_INNER) * jax.nn.silu(z.astype(f32))
    y = rms_norm(y, norm_w).astype(h.dtype)
    return y @ w_out


def window_attention(h, w_qkv, q_norm_w, k_norm_w, sink, w_out, pos):
    f32 = jnp.float32
    bsz, L, _ = h.shape
    hq, hk, dh, blk = ATTN_HEADS, ATTN_KV_HEADS, ATTN_HEAD_DIM, ATTN_BLOCK
    grp = hq // hk
    nb = L // blk
    q, k, v = jnp.split(h @ w_qkv, [hq * dh, (hq + hk) * dh], axis=-1)
    q = rotary(rms_norm(q.reshape(bsz, L, hq, dh), q_norm_w), pos)
    k = rotary(rms_norm(k.reshape(bsz, L, hk, dh), k_norm_w), pos)
    v = v.reshape(bsz, L, hk, dh)
    q = q.reshape(bsz, nb, blk, hk, grp, dh)

    def band(t):
        tp = jnp.pad(t, ((0, 0), (blk, blk), (0, 0), (0, 0))).reshape(bsz, nb + 2, blk, hk, dh)
        return jnp.concatenate([tp[:, :-2], tp[:, 1:-1], tp[:, 2:]], axis=2)

    kb, vb = band(k), band(v)
    scores = jnp.einsum('bnqkgd,bnskd->bnkgqs', q, kb).astype(f32) * (dh ** -0.5)
    qpos = jnp.arange(nb)[:, None] * blk + jnp.arange(blk)[None, :]
    kpos = jnp.arange(nb)[:, None] * blk - blk + jnp.arange(3 * blk)[None, :]
    rel = kpos[:, None, :] - qpos[:, :, None]
    valid = (jnp.abs(rel) <= WINDOW) & (kpos[:, None, :] >= 0) & (kpos[:, None, :] < L)
    scores = jnp.where(valid[None, :, None, None], scores, -jnp.inf)
    sink_col = jnp.broadcast_to(sink.astype(f32).reshape(hk, grp)[None, None, :, :, None, None],
                                scores.shape[:-1] + (1,))
    probs = jax.nn.softmax(jnp.concatenate([scores, sink_col], axis=-1), axis=-1)[..., :-1]
    out = jnp.einsum('bnkgqs,bnskd->bnqkgd', probs.astype(v.dtype), vb).reshape(bsz, L, hq * dh)
    return out @ w_out


def setup_inputs(seed: int = 0) -> dict:
    key = jax.random.key(seed)
    ks = jax.random.split(key, 24)
    f32 = jnp.float32
    nrm = lambda k, shape, fan_in: jax.random.normal(k, shape, f32) * (fan_in ** -0.5)
    x = jax.random.normal(ks[0], (BATCH, SEQ, D_MODEL), f32)
    norm_w = 1.0 + 0.02 * jax.random.normal(ks[1], (DEPTH, 3, D_MODEL), f32)
    ffn_w_gate = nrm(ks[2], (DEPTH, 2, D_MODEL, D_FF), D_MODEL)
    ffn_w_up = nrm(ks[3], (DEPTH, 2, D_MODEL, D_FF), D_MODEL)
    ffn_w_down = nrm(ks[4], (DEPTH, 2, D_FF, D_MODEL), D_FF)
    ns = N_SSD_LAYERS
    ssd_w_in = nrm(ks[5], (ns, D_MODEL, SSD_IN), D_MODEL)
    ssd_conv_w = nrm(ks[6], (ns, SSD_CONV, SSD_CONV_CH), SSD_CONV)
    ssd_conv_b = 0.02 * jax.random.normal(ks[7], (ns, SSD_CONV_CH), f32)
    dt0 = jnp.exp(jax.random.uniform(ks[8], (ns, 2, SSD_HEADS), f32, np.log(1e-3), np.log(1e-1)))
    ssd_dt_bias = dt0 + jnp.log(-jnp.expm1(-dt0))
    ssd_a_log = jnp.log(jax.random.uniform(ks[9], (ns, 2, SSD_HEADS), f32, 1.0, 16.0))
    ssd_d = 1.0 + 0.1 * jax.random.normal(ks[10], (ns, SSD_HEADS), f32)
    ssd_norm_w = 1.0 + 0.02 * jax.random.normal(ks[11], (ns, SSD_D_INNER), f32)
    ssd_w_out = nrm(ks[12], (ns, SSD_D_INNER, D_MODEL), SSD_D_INNER)
    na = N_ATTN_LAYERS
    attn_w_qkv = nrm(ks[13], (na, D_MODEL, ATTN_QKV), D_MODEL)
    attn_q_norm = 1.0 + 0.02 * jax.random.normal(ks[14], (na, ATTN_HEAD_DIM), f32)
    attn_k_norm = 1.0 + 0.02 * jax.random.normal(ks[15], (na, ATTN_HEAD_DIM), f32)
    attn_sink = 0.5 * jax.random.normal(ks[16], (na, ATTN_HEADS), f32)
    attn_w_out = nrm(ks[17], (na, ATTN_HEADS * ATTN_HEAD_DIM, D_MODEL), ATTN_HEADS * ATTN_HEAD_DIM)
    return {"x": x, "norm_w": norm_w, "ffn_w_gate": ffn_w_gate, "ffn_w_up": ffn_w_up,
            "ffn_w_down": ffn_w_down, "ssd_w_in": ssd_w_in, "ssd_conv_w": ssd_conv_w,
            "ssd_conv_b": ssd_conv_b, "ssd_dt_bias": ssd_dt_bias, "ssd_a_log": ssd_a_log,
            "ssd_d": ssd_d, "ssd_norm_w": ssd_norm_w, "ssd_w_out": ssd_w_out,
            "attn_w_qkv": attn_w_qkv, "attn_q_norm": attn_q_norm, "attn_k_norm": attn_k_norm,
            "attn_sink": attn_sink, "attn_w_out": attn_w_out}


def reference(x, norm_w, ffn_w_gate, ffn_w_up, ffn_w_down, ssd_w_in, ssd_conv_w, ssd_conv_b,
              ssd_dt_bias, ssd_a_log, ssd_d, ssd_norm_w, ssd_w_out, attn_w_qkv, attn_q_norm,
              attn_k_norm, attn_sink, attn_w_out):
    pos = jnp.arange(x.shape[1], dtype=jnp.int32)
    for i in range(DEPTH):
        x = x + 0.5 * swiglu(rms_norm(x, norm_w[i, 0]), ffn_w_gate[i, 0], ffn_w_up[i, 0], ffn_w_down[i, 0])
        h = rms_norm(x, norm_w[i, 1])
        j = i // N_MIXERS
        if i % N_MIXERS == 0:
            x = x + ssd_mixer(h, ssd_w_in[j], ssd_conv_w[j], ssd_conv_b[j], ssd_dt_bias[j],
                              ssd_a_log[j], ssd_d[j], ssd_norm_w[j], ssd_w_out[j])
        else:
            x = x + window_attention(h, attn_w_qkv[j], attn_q_norm[j], attn_k_norm[j],
                                     attn_sink[j], attn_w_out[j], pos)
        x = x + 0.5 * swiglu(rms_norm(x, norm_w[i, 2]), ffn_w_gate[i, 1], ffn_w_up[i, 1], ffn_w_down[i, 1])
    return x
```

```python
import functools

import jax
import jax.numpy as jnp
from jax import lax
from jax.experimental import pallas as pl
from jax.experimental.pallas import tpu as pltpu

F32 = jnp.float32
BF16 = jnp.bfloat16
EPS = 1e-6
ROPE_THETA = 10000.0

V7X_VMEM_LIMIT_BYTES = 56 * 1024 * 1024
LANES = 128
HALO = 8

SSD_CHUNK = 128
SSD_HEAD_DIM = 64
SSD_STATE = 128
SSD_GROUP_HEADS = 4
SSD_CONV = 5
ATTN_HEAD_DIM = 64
ATTN_BLOCK = 128

ROW_TILE = 512


def _params(*semantics):
    return pltpu.CompilerParams(dimension_semantics=semantics, vmem_limit_bytes=V7X_VMEM_LIMIT_BYTES)


def _resident(shape):
    return pl.BlockSpec(shape, lambda *_: (0,) * len(shape), pipeline_mode=pl.Buffered(1))


def _rmsnorm(x, w):
    return x * lax.rsqrt(jnp.mean(x * x, axis=-1, keepdims=True) + EPS) * w


def _dot(a, b):
    return jnp.dot(a, b, preferred_element_type=F32)


def _dot_nt(a, b):
    return lax.dot_general(a, b, (((1,), (1,)), ((), ())), preferred_element_type=F32)


def _dot_tn(a, b):
    return lax.dot_general(a, b, (((0,), (0,)), ((), ())), preferred_element_type=F32)


def _ffn_kernel(x_ref, nw_ref, wg_ref, wu_ref, wd_ref, o_ref, h_ref, *, ff_chunk):
    x = x_ref[...]
    xn = _rmsnorm(x, nw_ref[...]).astype(BF16)
    for c in range(wg_ref.shape[1] // ff_chunk):
        sl = slice(c * ff_chunk, (c + 1) * ff_chunk)
        g = _dot(xn, wg_ref[:, sl])
        u = _dot(xn, wu_ref[:, sl])
        h_ref[:, sl] = (jax.nn.silu(g) * u).astype(BF16)
    o_ref[...] = x + 0.5 * _dot(h_ref[...], wd_ref[...])


def _ffn(x, nw, wg, wu, wd):
    n, d = x.shape
    d_ff = wg.shape[1]
    tm = min(ROW_TILE, n)
    row = pl.BlockSpec((tm, d), lambda i: (i, 0))
    return pl.pallas_call(
        functools.partial(_ffn_kernel, ff_chunk=256),
        grid=(n // tm,),
        in_specs=[row, _resident((1, d)), _resident((d, d_ff)), _resident((d, d_ff)), _resident((d_ff, d))],
        out_specs=row,
        out_shape=jax.ShapeDtypeStruct((n, d), F32),
        scratch_shapes=[pltpu.VMEM((tm, d_ff), BF16)],
        compiler_params=_params("arbitrary"),
        name="ffn",
    )(x, nw, wg, wu, wd)


def _ssd_in_kernel(xp_ref, x_ref, xq_ref, nw_ref, wz_ref, wxbc_ref, wdt_ref, cw_ref, cb_ref, dtb_ref,
                   z_ref, xbc_ref, dt_ref, p_ref, *, blocks_per_seq, col_chunk):
    tm = x_ref.shape[0]
    j = lax.rem(pl.program_id(0), blocks_per_seq)
    nw = nw_ref[...]
    xm = _rmsnorm(x_ref[...], nw)
    xp = _rmsnorm(xp_ref[...], nw) * (j != 0).astype(F32)
    xq = _rmsnorm(xq_ref[...], nw) * (j != blocks_per_seq - 1).astype(F32)
    xm_bf = xm.astype(BF16)
    xe_bf = jnp.concatenate([xp, xm, xq], axis=0).astype(BF16)

    dt_ref[...] = jax.nn.softplus(_dot(xm_bf, wdt_ref[...]) + dtb_ref[...])
    for c in range(wz_ref.shape[1] // col_chunk):
        sl = slice(c * col_chunk, (c + 1) * col_chunk)
        z_ref[:, sl] = _dot(xm_bf, wz_ref[:, sl]).astype(BF16)
    for c in range(wxbc_ref.shape[1] // col_chunk):
        sl = slice(c * col_chunk, (c + 1) * col_chunk)
        p_ref[...] = _dot(xe_bf, wxbc_ref[:, sl])
        acc = cb_ref[:, sl]
        for k in range(SSD_CONV):
            acc = acc + cw_ref[k:k + 1, sl] * p_ref[pl.ds(HALO - SSD_CONV // 2 + k, tm), :]
        xbc_ref[:, sl] = jax.nn.silu(acc).astype(BF16)


def _ssd_in(x, nw, wz, wxbc, wdt, cw, cb, dtb, *, seq):
    n, d = x.shape
    tm = min(ROW_TILE, seq)
    bps = seq // tm
    hb = tm // HALO
    last_halo = n // HALO - 1
    row = lambda w: pl.BlockSpec((tm, w), lambda i: (i, 0))
    prev = pl.BlockSpec((HALO, d), lambda i: (jnp.maximum(i * hb - 1, 0), 0))
    nxt = pl.BlockSpec((HALO, d), lambda i: (jnp.minimum((i + 1) * hb, last_halo), 0))
    col_chunk = 512
    return pl.pallas_call(
        functools.partial(_ssd_in_kernel, blocks_per_seq=bps, col_chunk=col_chunk),
        grid=(n // tm,),
        in_specs=[prev, row(d), nxt, _resident((1, d)), _resident(wz.shape), _resident(wxbc.shape),
                  _resident(wdt.shape), _resident(cw.shape), _resident(cb.shape), _resident(dtb.shape)],
        out_specs=[row(wz.shape[1]), row(wxbc.shape[1]), row(LANES)],
        out_shape=[jax.ShapeDtypeStruct((n, wz.shape[1]), BF16),
                   jax.ShapeDtypeStruct((n, wxbc.shape[1]), BF16),
                   jax.ShapeDtypeStruct((n, LANES), F32)],
        scratch_shapes=[pltpu.VMEM((tm + 2 * HALO, col_chunk), F32)],
        compiler_params=_params("arbitrary"),
        name="ssd_in",
    )(x, x, x, nw, wz, wxbc, wdt, cw, cb, dtb)


def _chunk_decays(dt, a_row, tril, n_heads):
    t = dt.shape[0]
    da = dt * a_row
    prefix = jnp.dot(tril, da, precision=lax.Precision.HIGHEST, preferred_element_type=F32)
    total = prefix[t - 1:t, :]
    lane = lax.broadcasted_iota(jnp.int32, dt.shape, 1)
    cum = jnp.where(lane < n_heads, prefix, total - prefix + da)
    return cum, jnp.exp(cum), dt * jnp.exp(total - cum)


def _bcol(v, j):
    return jnp.broadcast_to(v[:, j:j + 1], (v.shape[0], LANES))


def _group_cols(v, h0):
    lane = lax.broadcasted_iota(jnp.int32, (v.shape[0], LANES), 1)
    first = lane < SSD_HEAD_DIM
    return jnp.concatenate([jnp.where(first, _bcol(v, h0), _bcol(v, h0 + 1)),
                            jnp.where(first, _bcol(v, h0 + 2), _bcol(v, h0 + 3))], axis=1)


def _ssd_bwd_kernel(xbc_ref, dt_ref, alog_ref, yb_ref, st_ref, *, n_heads, n_groups):
    t = SSD_CHUNK
    gw = SSD_GROUP_HEADS * SSD_HEAD_DIM
    d_inner = n_heads * SSD_HEAD_DIM
    n_chunks = xbc_ref.shape[0] // t

    @pl.when(pl.program_id(1) == 0)
    def _():
        st_ref[...] = jnp.zeros_like(st_ref)

    a_row = -jnp.exp(alog_ref[...])
    ri = lax.broadcasted_iota(jnp.int32, (t, t), 0)
    ci = lax.broadcasted_iota(jnp.int32, (t, t), 1)
    tril = (ri >= ci).astype(F32)

    def chunk(i, carry):
        r0 = pl.multiple_of((n_chunks - 1 - i) * t, t)
        rows = pl.ds(r0, t)
        _, ecum, ws = _chunk_decays(dt_ref[rows, :], a_row, tril, n_heads)
        for g in range(n_groups):
            h0 = n_heads + g * SSD_GROUP_HEADS
            xg = xbc_ref[rows, g * gw:(g + 1) * gw]
            bg = xbc_ref[rows, d_inner + g * SSD_STATE:d_inner + (g + 1) * SSD_STATE]
            cg = xbc_ref[rows, d_inner + (n_groups + g) * SSD_STATE:d_inner + (n_groups + g + 1) * SSD_STATE]
            e_out = _group_cols(ecum, h0)
            st = st_ref[g]
            yb_ref[rows, g * gw:(g + 1) * gw] = _dot(cg, st.astype(BF16)) * e_out
            xw = (xg.astype(F32) * _group_cols(ws, h0)).astype(BF16)
            st_ref[g] = st * e_out[0:1, :] + _dot_tn(bg, xw)
        return carry

    lax.fori_loop(0, n_chunks, chunk, 0)


def _ssd_main_kernel(xbc_ref, dt_ref, z_ref, yb_ref, x_ref, alog_ref, dexp_ref, nw_ref, wout_ref,
                     o_ref, st_ref, y_ref, *, n_heads, n_groups):
    t = SSD_CHUNK
    p = SSD_HEAD_DIM
    gh = SSD_GROUP_HEADS
    gw = gh * p
    d_inner = n_heads * p
    n_chunks = xbc_ref.shape[0] // t

    @pl.when(pl.program_id(1) == 0)
    def _():
        st_ref[...] = jnp.zeros_like(st_ref)

    a_row = -jnp.exp(alog_ref[...])
    ri = lax.broadcasted_iota(jnp.int32, (t, t), 0)
    ci = lax.broadcasted_iota(jnp.int32, (t, t), 1)
    lower = ri >= ci
    tril = lower.astype(F32)
    lane_g = lax.broadcasted_iota(jnp.int32, (t, gw), 1)

    def chunk(i, carry):
        r0 = pl.multiple_of(i * t, t)
        rows = pl.ds(r0, t)
        dt = dt_ref[rows, :]
        cum, ecum, ws = _chunk_decays(dt, a_row, tril, n_heads)
        cum_t = cum.T
        dt_t = dt.T
        for g in range(n_groups):
            xg = xbc_ref[rows, g * gw:(g + 1) * gw]
            bg = xbc_ref[rows, d_inner + g * SSD_STATE:d_inner + (g + 1) * SSD_STATE]
            cg = xbc_ref[rows, d_inner + (n_groups + g) * SSD_STATE:d_inner + (n_groups + g + 1) * SSD_STATE]
            cb = _dot_nt(cg, bg)
            ms, xs = [], []
            for e in range(gh):
                hf = g * gh + e
                hb = n_heads + hf
                arg = jnp.where(lower, _bcol(cum, hf) - cum_t[hf:hf + 1, :], _bcol(cum, hb) - cum_t[hb:hb + 1, :])
                dtf = dt_t[hf:hf + 1, :]
                dtb = dt_t[hb:hb + 1, :]
                w = jnp.where(ri > ci, dtf, jnp.where(ri < ci, dtb, dtf + dtb))
                ms.append((cb * jnp.exp(arg) * w).astype(BF16))
                xs.append(jnp.where((lane_g >= e * p) & (lane_g < (e + 1) * p), xg, jnp.zeros_like(xg)))
            y = _dot(jnp.concatenate(ms, axis=1), jnp.concatenate(xs, axis=0))
            xf = xg.astype(F32)
            y = y + xf * dexp_ref[:, g * gw:(g + 1) * gw]
            e_out = _group_cols(ecum, g * gh)
            st = st_ref[g]
            y = y + _dot(cg, st.astype(BF16)) * e_out
            y_ref[rows, g * gw:(g + 1) * gw] = y
            xw = (xf * _group_cols(ws, g * gh)).astype(BF16)
            st_ref[g] = st * e_out[t - 1:t, :] + _dot_tn(bg, xw)
        return carry

    lax.fori_loop(0, n_chunks, chunk, 0)

    y = (y_ref[...] + yb_ref[...]) * jax.nn.silu(z_ref[...].astype(F32))
    o_ref[...] = x_ref[...] + _dot(_rmsnorm(y, nw_ref[...]).astype(BF16), wout_ref[...])


def _ssd_scan(x, xbc, dt, z, alog, dexp, nw, wout, *, batch, seq, n_heads, n_groups):
    n, d = x.shape
    d_inner = n_heads * SSD_HEAD_DIM
    tb = min(ROW_TILE, seq)
    bps = seq // tb
    state = pltpu.VMEM((n_groups, SSD_STATE, SSD_GROUP_HEADS * SSD_HEAD_DIM), F32)
    rev = lambda w: pl.BlockSpec((tb, w), lambda b, j: (b * bps + bps - 1 - j, 0))
    fwd = lambda w: pl.BlockSpec((tb, w), lambda b, j: (b * bps + j, 0))

    yb = pl.pallas_call(
        functools.partial(_ssd_bwd_kernel, n_heads=n_heads, n_groups=n_groups),
        grid=(batch, bps),
        in_specs=[rev(xbc.shape[1]), rev(LANES), _resident(alog.shape)],
        out_specs=rev(d_inner),
        out_shape=jax.ShapeDtypeStruct((n, d_inner), F32),
        scratch_shapes=[state],
        compiler_params=_params("arbitrary", "arbitrary"),
        name="ssd_bwd",
    )(xbc, dt, alog)

    return pl.pallas_call(
        functools.partial(_ssd_main_kernel, n_heads=n_heads, n_groups=n_groups),
        grid=(batch, bps),
        in_specs=[fwd(xbc.shape[1]), fwd(LANES), fwd(d_inner), fwd(d_inner), fwd(d),
                  _resident(alog.shape), _resident(dexp.shape), _resident(nw.shape), _resident(wout.shape)],
        out_specs=fwd(d),
        out_shape=jax.ShapeDtypeStruct((n, d), F32),
        scratch_shapes=[state, pltpu.VMEM((tb, d_inner), F32)],
        compiler_params=_params("arbitrary", "arbitrary"),
        name="ssd_main",
    )(xbc, dt, z, yb, x, alog, dexp, nw, wout)


def _attn_qkv_kernel(x_ref, nw_ref, wq_ref, wk_ref, wv_ref, qn_ref, kn_ref, cos_ref, sin_ref,
                     q_ref, k_ref, v_ref):
    xn = _rmsnorm(x_ref[...], nw_ref[...]).astype(BF16)
    cos = cos_ref[...]
    sin = sin_ref[...]
    lane = lax.broadcasted_iota(jnp.int32, cos.shape, 1)
    first = lane < ATTN_HEAD_DIM
    low_half = lax.rem(lane, ATTN_HEAD_DIM) < ATTN_HEAD_DIM // 2

    def norm_rope(w_ref_, norm_w, out_ref, scale):
        for v in range(w_ref_.shape[1] // LANES):
            sl = slice(v * LANES, (v + 1) * LANES)
            tv = _dot(xn, w_ref_[:, sl])
            sq = tv * tv
            s0 = jnp.sum(jnp.where(first, sq, 0.0), axis=-1, keepdims=True)
            s1 = jnp.sum(jnp.where(first, 0.0, sq), axis=-1, keepdims=True)
            inv = 1.0 / ATTN_HEAD_DIM
            rs = jnp.where(first, lax.rsqrt(s0 * inv + EPS), lax.rsqrt(s1 * inv + EPS))
            tn = tv * rs * norm_w
            rot = jnp.where(low_half, pltpu.roll(tn, LANES - ATTN_HEAD_DIM // 2, axis=1),
                            pltpu.roll(tn, ATTN_HEAD_DIM // 2, axis=1))
            out_ref[:, sl] = ((tn * cos + rot * sin) * scale).astype(BF16)

    norm_rope(wq_ref, qn_ref[...], q_ref, ATTN_HEAD_DIM ** -0.5)
    norm_rope(wk_ref, kn_ref[...], k_ref, 1.0)
    v_ref[...] = _dot(xn, wv_ref[...]).astype(BF16)


def _attn_core_kernel(sink_ref, q_ref, kp_ref, k_ref, kq_ref, vp_ref, v_ref, vq_ref, x_ref, wout_ref,
                      o_ref, kbuf_ref, vbuf_ref, obuf_ref, *, blocks_per_seq, n_heads, n_kv):
    blk = ATTN_BLOCK
    dh = ATTN_HEAD_DIM
    tq = q_ref.shape[0]
    n_sub = tq // blk
    j = lax.rem(pl.program_id(0), blocks_per_seq)
    kbuf_ref[0:blk, :] = kp_ref[...]
    kbuf_ref[blk:blk + tq, :] = k_ref[...]
    kbuf_ref[blk + tq:, :] = kq_ref[...]
    vbuf_ref[0:blk, :] = vp_ref[...]
    vbuf_ref[blk:blk + tq, :] = v_ref[...]
    vbuf_ref[blk + tq:, :] = vq_ref[...]

    qi = lax.broadcasted_iota(jnp.int32, (blk, 3 * blk), 0)
    si = lax.broadcasted_iota(jnp.int32, (blk, 3 * blk), 1)
    rel = si - blk - qi
    band = (rel <= blk) & (rel >= -blk)

    def sub(i, carry):
        r0 = pl.multiple_of(i * blk, blk)
        no_prev = (j == 0) & (i == 0)
        no_next = (j == blocks_per_seq - 1) & (i == n_sub - 1)
        valid = band & jnp.logical_not(no_prev & (si < blk)) & jnp.logical_not(no_next & (si >= 2 * blk))
        for g in range(n_kv):
            kg = kbuf_ref[pl.ds(r0, 3 * blk), g * dh:(g + 1) * dh]
            vg = vbuf_ref[pl.ds(r0, 3 * blk), g * dh:(g + 1) * dh]
            for e in range(n_heads // n_kv):
                h = g * (n_heads // n_kv) + e
                s = _dot_nt(q_ref[pl.ds(r0, blk), h * dh:(h + 1) * dh], kg)
                s = jnp.where(valid, s, -jnp.inf)
                sink = sink_ref[h]
                m = jnp.maximum(jnp.max(s, axis=-1, keepdims=True), sink)
                pr = jnp.exp(s - m)
                den = jnp.sum(pr, axis=-1, keepdims=True) + jnp.exp(sink - m)
                obuf_ref[pl.ds(r0, blk), h * dh:(h + 1) * dh] = (_dot(pr.astype(BF16), vg) / den).astype(BF16)
        return carry

    lax.fori_loop(0, n_sub, sub, 0)
    o_ref[...] = x_ref[...] + _dot(obuf_ref[...], wout_ref[...])


def _attention(x, nw, wq, wk, wv, qn, kn, cos, sin, sink, wout, *, seq, n_heads, n_kv):
    n, d = x.shape
    tm = min(ROW_TILE, seq)
    bps = seq // tm
    dq, dk = wq.shape[1], wk.shape[1]
    row = lambda w: pl.BlockSpec((tm, w), lambda i: (i, 0))
    pos = pl.BlockSpec((tm, LANES), lambda i: (lax.rem(i, bps), 0))
    q, k, v = pl.pallas_call(
        _attn_qkv_kernel,
        grid=(n // tm,),
        in_specs=[row(d), _resident((1, d)), _resident(wq.shape), _resident(wk.shape), _resident(wv.shape),
                  _resident(qn.shape), _resident(kn.shape), pos, pos],
        out_specs=[row(dq), row(dk), row(dk)],
        out_shape=[jax.ShapeDtypeStruct((n, dq), BF16), jax.ShapeDtypeStruct((n, dk), BF16),
                   jax.ShapeDtypeStruct((n, dk), BF16)],
        compiler_params=_params("arbitrary"),
        name="attn_qkv",
    )(x, nw, wq, wk, wv, qn, kn, cos, sin)

    sb = tm // ATTN_BLOCK
    last = n // ATTN_BLOCK - 1
    prev = pl.BlockSpec((ATTN_BLOCK, dk), lambda i: (jnp.maximum(i * sb - 1, 0), 0))
    nxt = pl.BlockSpec((ATTN_BLOCK, dk), lambda i: (jnp.minimum((i + 1) * sb, last), 0))
    return pl.pallas_call(
        functools.partial(_attn_core_kernel, blocks_per_seq=bps, n_heads=n_heads, n_kv=n_kv),
        grid=(n // tm,),
        in_specs=[pl.BlockSpec(memory_space=pltpu.SMEM), row(dq), prev, row(dk), nxt, prev, row(dk), nxt,
                  row(d), _resident(wout.shape)],
        out_specs=row(d),
        out_shape=jax.ShapeDtypeStruct((n, d), F32),
        scratch_shapes=[pltpu.VMEM((tm + 2 * ATTN_BLOCK, dk), BF16), pltpu.VMEM((tm + 2 * ATTN_BLOCK, dk), BF16),
                        pltpu.VMEM((tm, dq), BF16)],
        compiler_params=_params("arbitrary"),
        name="attn_core",
    )(sink, q, k, k, k, v, v, v, x, wout)


def _rope_tables(seq):
    half = ATTN_HEAD_DIM // 2
    inv_freq = ROPE_THETA ** (-jnp.arange(half, dtype=F32) / half)
    ang = jnp.arange(seq, dtype=jnp.int32).astype(F32)[:, None] * inv_freq[None, :]
    cos = jnp.tile(jnp.cos(ang), (1, 2 * LANES // ATTN_HEAD_DIM))
    sin = jnp.sin(ang)
    sin = jnp.tile(jnp.concatenate([-sin, sin], axis=1), (1, LANES // ATTN_HEAD_DIM))
    return cos, sin


def kernel(x, norm_w, ffn_w_gate, ffn_w_up, ffn_w_down, ssd_w_in, ssd_conv_w, ssd_conv_b, ssd_dt_bias,
           ssd_a_log, ssd_d, ssd_norm_w, ssd_w_out, attn_w_qkv, attn_q_norm, attn_k_norm, attn_sink,
           attn_w_out):
    batch, seq, d = x.shape
    depth = norm_w.shape[0]
    n_ssd_heads = ssd_d.shape[1]
    d_inner = n_ssd_heads * SSD_HEAD_DIM
    conv_ch = ssd_conv_w.shape[2]
    n_groups = (conv_ch - d_inner) // (2 * SSD_STATE)
    n_heads = attn_sink.shape[1]
    dq = n_heads * ATTN_HEAD_DIM
    dk = (attn_w_qkv.shape[2] - dq) // 2
    n_kv = dk // ATTN_HEAD_DIM
    bf = lambda w: w.astype(BF16)
    pad_lanes = lambda v: jnp.pad(v.astype(F32).reshape(1, -1), ((0, 0), (0, LANES - v.size)))
    cos, sin = _rope_tables(seq)

    h = x.reshape(batch * seq, d)
    for i in range(depth):
        h = _ffn(h, norm_w[i, 0][None], bf(ffn_w_gate[i, 0]), bf(ffn_w_up[i, 0]), bf(ffn_w_down[i, 0]))
        nw = norm_w[i, 1][None]
        j = i // 2
        if i % 2 == 0:
            w_in = ssd_w_in[j]
            wdt = jnp.pad(w_in[:, d_inner + conv_ch:], ((0, 0), (0, LANES - 2 * n_ssd_heads)))
            z, xbc, dt = _ssd_in(h, nw, bf(w_in[:, :d_inner]), bf(w_in[:, d_inner:d_inner + conv_ch]), bf(wdt),
                                 ssd_conv_w[j], ssd_conv_b[j][None], pad_lanes(ssd_dt_bias[j]), seq=seq)
            h = _ssd_scan(h, xbc, dt, z, pad_lanes(ssd_a_log[j]),
                          jnp.repeat(ssd_d[j], SSD_HEAD_DIM)[None], ssd_norm_w[j][None], bf(ssd_w_out[j]),
                          batch=batch, seq=seq, n_heads=n_ssd_heads, n_groups=n_groups)
        else:
            w = attn_w_qkv[j]
            tile2 = lambda v: jnp.tile(v, LANES // ATTN_HEAD_DIM)[None]
            h = _attention(h, nw, bf(w[:, :dq]), bf(w[:, dq:dq + dk]), bf(w[:, dq + dk:]),
                           tile2(attn_q_norm[j]), tile2(attn_k_norm[j]), cos, sin, attn_sink[j],
                           bf(attn_w_out[j]), seq=seq, n_heads=n_heads, n_kv=n_kv)
        h = _ffn(h, norm_w[i, 2][None], bf(ffn_w_gate[i, 1]), bf(ffn_w_up[i, 1]), bf(ffn_w_down[i, 1]))
    return h.reshape(batch, seq, d)
```

```python
import functools

import jax
import jax.numpy as jnp
from jax import lax
from jax.experimental import pallas as pl
from jax.experimental.pallas import tpu as pltpu

F32 = jnp.float32
BF16 = jnp.bfloat16
EPS = 1e-6
ROPE_THETA = 10000.0

V7X_VMEM_LIMIT_BYTES = 56 * 1024 * 1024
LANES = 128
HALO = 8

SSD_CHUNK = 128
SSD_HEAD_DIM = 64
SSD_STATE = 128
SSD_GROUP_HEADS = 4
SSD_CONV = 5
ATTN_HEAD_DIM = 64
ATTN_BLOCK = 128

ROW_TILE = 512


def _params(*semantics):
    return pltpu.CompilerParams(dimension_semantics=semantics, vmem_limit_bytes=V7X_VMEM_LIMIT_BYTES)


def _resident(shape):
    return pl.BlockSpec(shape, lambda *_: (0,) * len(shape), pipeline_mode=pl.Buffered(1))


def _rmsnorm(x, w):
    return x * lax.rsqrt(jnp.mean(x * x, axis=-1, keepdims=True) + EPS) * w


def _silu(x):
    h = 0.5 * x
    return h + h * jnp.tanh(h)


def _dot(a, b):
    return jnp.dot(a, b, preferred_element_type=F32)


def _dot_nt(a, b):
    return lax.dot_general(a, b, (((1,), (1,)), ((), ())), preferred_element_type=F32)


def _dot_tn(a, b):
    return lax.dot_general(a, b, (((0,), (0,)), ((), ())), preferred_element_type=F32)


def _ffn_kernel(x_ref, nw_ref, wg_ref, wu_ref, wd_ref, o_ref, h_ref, *, ff_chunk):
    x = x_ref[...]
    xn = _rmsnorm(x, nw_ref[...]).astype(BF16)
    for c in range(wg_ref.shape[1] // ff_chunk):
        sl = slice(c * ff_chunk, (c + 1) * ff_chunk)
        g = _dot(xn, wg_ref[:, sl])
        u = _dot(xn, wu_ref[:, sl])
        h_ref[:, sl] = (_silu(g) * u).astype(BF16)
    o_ref[...] = x + 0.5 * _dot(h_ref[...], wd_ref[...])


def _ffn(x, nw, wg, wu, wd):
    n, d = x.shape
    d_ff = wg.shape[1]
    tm = min(ROW_TILE, n)
    row = pl.BlockSpec((tm, d), lambda i: (i, 0))
    return pl.pallas_call(
        functools.partial(_ffn_kernel, ff_chunk=256),
        grid=(n // tm,),
        in_specs=[row, _resident((1, d)), _resident((d, d_ff)), _resident((d, d_ff)), _resident((d_ff, d))],
        out_specs=row,
        out_shape=jax.ShapeDtypeStruct((n, d), F32),
        scratch_shapes=[pltpu.VMEM((tm, d_ff), BF16)],
        compiler_params=_params("arbitrary"),
        name="ffn",
    )(x, nw, wg, wu, wd)


def _ssd_in_kernel(xp_ref, x_ref, xq_ref, nw_ref, wz_ref, wxbc_ref, wdt_ref, cw_ref, cb_ref, dtb_ref,
                   z_ref, xbc_ref, dt_ref, *, blocks_per_seq, col_chunk):
    tm = x_ref.shape[0]
    rows_ext = tm + 2 * HALO
    j = lax.rem(pl.program_id(0), blocks_per_seq)
    nw = nw_ref[...]
    xm = _rmsnorm(x_ref[...], nw)
    xp = _rmsnorm(xp_ref[...], nw) * (j != 0).astype(F32)
    xq = _rmsnorm(xq_ref[...], nw) * (j != blocks_per_seq - 1).astype(F32)
    xm_bf = xm.astype(BF16)
    xe_bf = jnp.concatenate([xp, xm, xq], axis=0).astype(BF16)

    dt_ref[...] = jax.nn.softplus(_dot(xm_bf, wdt_ref[...]) + dtb_ref[...])
    for c in range(wz_ref.shape[1] // col_chunk):
        sl = slice(c * col_chunk, (c + 1) * col_chunk)
        z_ref[:, sl] = _dot(xm_bf, wz_ref[:, sl]).astype(BF16)
    for c in range(wxbc_ref.shape[1] // col_chunk):
        sl = slice(c * col_chunk, (c + 1) * col_chunk)
        u = _dot(xe_bf, wxbc_ref[:, sl])
        a = cw_ref[SSD_CONV - 1:SSD_CONV, sl] * u
        for k in range(SSD_CONV - 2, -1, -1):
            a = cw_ref[k:k + 1, sl] * u + pltpu.roll(a, rows_ext - 1, axis=0)
        a = pltpu.roll(a, rows_ext - (HALO - SSD_CONV // 2), axis=0)[:tm]
        xbc_ref[:, sl] = _silu(a + cb_ref[:, sl]).astype(BF16)


def _ssd_in(x, nw, wz, wxbc, wdt, cw, cb, dtb, *, seq):
    n, d = x.shape
    tm = min(ROW_TILE, seq)
    bps = seq // tm
    hb = tm // HALO
    last_halo = n // HALO - 1
    row = lambda w: pl.BlockSpec((tm, w), lambda i: (i, 0))
    prev = pl.BlockSpec((HALO, d), lambda i: (jnp.maximum(i * hb - 1, 0), 0))
    nxt = pl.BlockSpec((HALO, d), lambda i: (jnp.minimum((i + 1) * hb, last_halo), 0))
    col_chunk = 512
    return pl.pallas_call(
        functools.partial(_ssd_in_kernel, blocks_per_seq=bps, col_chunk=col_chunk),
        grid=(n // tm,),
        in_specs=[prev, row(d), nxt, _resident((1, d)), _resident(wz.shape), _resident(wxbc.shape),
                  _resident(wdt.shape), _resident(cw.shape), _resident(cb.shape), _resident(dtb.shape)],
        out_specs=[row(wz.shape[1]), row(wxbc.shape[1]), row(LANES)],
        out_shape=[jax.ShapeDtypeStruct((n, wz.shape[1]), BF16),
                   jax.ShapeDtypeStruct((n, wxbc.shape[1]), BF16),
                   jax.ShapeDtypeStruct((n, LANES), F32)],
        compiler_params=_params("arbitrary"),
        name="ssd_in",
    )(x, x, x, nw, wz, wxbc, wdt, cw, cb, dtb)


def _chunk_decays(dt, a_row, tril, n_heads):
    t = dt.shape[0]
    da = dt * a_row
    prefix = jnp.dot(tril, da, precision=lax.Precision.HIGHEST, preferred_element_type=F32)
    total = prefix[t - 1:t, :]
    lane = lax.broadcasted_iota(jnp.int32, dt.shape, 1)
    return jnp.where(lane < n_heads, prefix, total - prefix + da), total


def _bcol(v, j):
    return jnp.broadcast_to(v[:, j:j + 1], (v.shape[0], LANES))


def _head_lanes(cols):
    first = lax.broadcasted_iota(jnp.int32, cols[0].shape, 1) < SSD_HEAD_DIM
    return jnp.concatenate([jnp.where(first, cols[0], cols[1]), jnp.where(first, cols[2], cols[3])], axis=1)


def _block_diag(xg):
    lane = lax.broadcasted_iota(jnp.int32, xg.shape, 1)
    zero = jnp.zeros_like(xg)
    return jnp.concatenate(
        [jnp.where((lane >= e * SSD_HEAD_DIM) & (lane < (e + 1) * SSD_HEAD_DIM), xg, zero)
         for e in range(SSD_GROUP_HEADS)], axis=0)


def _ssd_bwd_kernel(xbc_ref, dt_ref, alog_ref, yb_ref, st_ref, *, n_heads, n_groups):
    t = SSD_CHUNK
    gh = SSD_GROUP_HEADS
    gw = gh * SSD_HEAD_DIM
    d_inner = n_heads * SSD_HEAD_DIM
    n_chunks = xbc_ref.shape[0] // t

    @pl.when(pl.program_id(1) == 0)
    def _():
        st_ref[...] = jnp.zeros_like(st_ref)

    a_row = -jnp.exp(alog_ref[...])
    ri = lax.broadcasted_iota(jnp.int32, (t, t), 0)
    ci = lax.broadcasted_iota(jnp.int32, (t, t), 1)
    tril = (ri >= ci).astype(F32)

    def chunk(i, carry):
        r0 = pl.multiple_of((n_chunks - 1 - i) * t, t)
        rows = pl.ds(r0, t)
        dt = dt_ref[rows, :]
        cum, total = _chunk_decays(dt, a_row, tril, n_heads)
        ws_t = (dt * jnp.exp(total - cum)).T
        decay = jnp.exp(total)
        for g in range(n_groups):
            h0 = n_heads + g * gh
            xg = xbc_ref[rows, g * gw:(g + 1) * gw]
            bg = xbc_ref[rows, d_inner + g * SSD_STATE:d_inner + (g + 1) * SSD_STATE]
            cg = xbc_ref[rows, d_inner + (n_groups + g) * SSD_STATE:d_inner + (n_groups + g + 1) * SSD_STATE]
            st = st_ref[g]
            yb_ref[rows, g * gw:(g + 1) * gw] = _dot(cg, st.astype(BF16))
            bt = bg.astype(F32).T
            bws = [(bt * ws_t[h0 + e:h0 + e + 1, :]).astype(BF16) for e in range(gh)]
            st_ref[g] = (st * _head_lanes([_bcol(decay, h0 + e) for e in range(gh)])
                         + _dot(jnp.concatenate(bws, axis=1), _block_diag(xg)))
        return carry

    lax.fori_loop(0, n_chunks, chunk, 0)


def _ssd_main_kernel(xbc_ref, dt_ref, z_ref, yb_ref, x_ref, alog_ref, dexp_ref, nw_ref, wout_ref,
                     o_ref, st_ref, y_ref, *, n_heads, n_groups):
    t = SSD_CHUNK
    gh = SSD_GROUP_HEADS
    gw = gh * SSD_HEAD_DIM
    d_inner = n_heads * SSD_HEAD_DIM
    n_chunks = xbc_ref.shape[0] // t

    @pl.when(pl.program_id(1) == 0)
    def _():
        st_ref[...] = jnp.zeros_like(st_ref)

    a_row = -jnp.exp(alog_ref[...])
    ri = lax.broadcasted_iota(jnp.int32, (t, t), 0)
    ci = lax.broadcasted_iota(jnp.int32, (t, t), 1)
    lower = ri >= ci
    tril = lower.astype(F32)

    def chunk(i, carry):
        r0 = pl.multiple_of(i * t, t)
        rows = pl.ds(r0, t)
        dt = dt_ref[rows, :]
        cum, total = _chunk_decays(dt, a_row, tril, n_heads)
        cum_t = cum.T
        dt_t = dt.T
        ws_t = (dt * jnp.exp(total - cum)).T
        for g in range(n_groups):
            cols = slice(g * gw, (g + 1) * gw)
            xg = xbc_ref[rows, cols]
            bg = xbc_ref[rows, d_inner + g * SSD_STATE:d_inner + (g + 1) * SSD_STATE]
            cg = xbc_ref[rows, d_inner + (n_groups + g) * SSD_STATE:d_inner + (n_groups + g + 1) * SSD_STATE]
            bt = bg.astype(F32).T
            cb = _dot(cg, bt.astype(BF16))
            ms, bws, ef, eb = [], [], [], []
            for e in range(gh):
                hf = g * gh + e
                hb = n_heads + hf
                cf_col = _bcol(cum, hf)
                cb_col = _bcol(cum, hb)
                arg = jnp.where(lower, cf_col - cum_t[hf:hf + 1, :], cb_col - cum_t[hb:hb + 1, :])
                dtf = dt_t[hf:hf + 1, :]
                dtb = dt_t[hb:hb + 1, :]
                w = jnp.where(ri > ci, dtf, jnp.where(ri < ci, dtb, dtf + dtb))
                ms.append((cb * jnp.exp(arg) * w).astype(BF16))
                bws.append((bt * ws_t[hf:hf + 1, :]).astype(BF16))
                ef.append(jnp.exp(cf_col))
                eb.append(jnp.exp(cb_col))
            lhs = jnp.concatenate([jnp.concatenate(ms, axis=1), jnp.concatenate(bws, axis=1)], axis=0)
            r = _dot(lhs, _block_diag(xg))
            e_f = _head_lanes(ef)
            st = st_ref[g]
            y_ref[rows, cols] = (r[:t] + xg.astype(F32) * dexp_ref[:, cols] + _dot(cg, st.astype(BF16)) * e_f
                                 + yb_ref[rows, cols] * _head_lanes(eb))
            st_ref[g] = st * e_f[t - 1:t, :] + r[t:]
        return carry

    lax.fori_loop(0, n_chunks, chunk, 0)

    y = y_ref[...] * _silu(z_ref[...].astype(F32))
    o_ref[...] = x_ref[...] + _dot(_rmsnorm(y, nw_ref[...]).astype(BF16), wout_ref[...])


def _ssd_scan(x, xbc, dt, z, alog, dexp, nw, wout, *, batch, seq, n_heads, n_groups):
    n, d = x.shape
    d_inner = n_heads * SSD_HEAD_DIM
    tb = min(ROW_TILE, seq)
    bps = seq // tb
    state = pltpu.VMEM((n_groups, SSD_STATE, SSD_GROUP_HEADS * SSD_HEAD_DIM), F32)
    rev = lambda w: pl.BlockSpec((tb, w), lambda b, j: (b * bps + bps - 1 - j, 0))
    fwd = lambda w: pl.BlockSpec((tb, w), lambda b, j: (b * bps + j, 0))

    yb = pl.pallas_call(
        functools.partial(_ssd_bwd_kernel, n_heads=n_heads, n_groups=n_groups),
        grid=(batch, bps),
        in_specs=[rev(xbc.shape[1]), rev(LANES), _resident(alog.shape)],
        out_specs=rev(d_inner),
        out_shape=jax.ShapeDtypeStruct((n, d_inner), F32),
        scratch_shapes=[state],
        compiler_params=_params("arbitrary", "arbitrary"),
        name="ssd_bwd",
    )(xbc, dt, alog)

    return pl.pallas_call(
        functools.partial(_ssd_main_kernel, n_heads=n_heads, n_groups=n_groups),
        grid=(batch, bps),
        in_specs=[fwd(xbc.shape[1]), fwd(LANES), fwd(d_inner), fwd(d_inner), fwd(d),
                  _resident(alog.shape), _resident(dexp.shape), _resident(nw.shape), _resident(wout.shape)],
        out_specs=fwd(d),
        out_shape=jax.ShapeDtypeStruct((n, d), F32),
        scratch_shapes=[state, pltpu.VMEM((tb, d_inner), F32)],
        compiler_params=_params("arbitrary", "arbitrary"),
        name="ssd_main",
    )(xbc, dt, z, yb, x, alog, dexp, nw, wout)


def _attn_qkv_kernel(x_ref, nw_ref, wq_ref, wkt_ref, wv_ref, qn_ref, knt_ref, cos_ref, sin_ref, cost_ref,
                     sint_ref, q_ref, kt_ref, v_ref):
    dh = ATTN_HEAD_DIM
    xn = _rmsnorm(x_ref[...], nw_ref[...]).astype(BF16)

    cos = cos_ref[...]
    sin = sin_ref[...]
    qn = qn_ref[...]
    lane = lax.broadcasted_iota(jnp.int32, cos.shape, 1)
    first = lane < dh
    low_half = lax.rem(lane, dh) < dh // 2
    wide = 2 * LANES
    for c in range(wq_ref.shape[1] // wide):
        tq = _dot(xn, wq_ref[:, c * wide:(c + 1) * wide])
        for v in range(wide // LANES):
            tv = tq[:, v * LANES:(v + 1) * LANES]
            sq = tv * tv
            s0 = jnp.sum(jnp.where(first, sq, 0.0), axis=-1, keepdims=True)
            s1 = jnp.sum(jnp.where(first, 0.0, sq), axis=-1, keepdims=True)
            rs = jnp.where(first, lax.rsqrt(s0 * (1.0 / dh) + EPS), lax.rsqrt(s1 * (1.0 / dh) + EPS))
            tn = tv * rs * qn
            rot = jnp.where(low_half, pltpu.roll(tn, LANES - dh // 2, axis=1), pltpu.roll(tn, dh // 2, axis=1))
            q_ref[:, c * wide + v * LANES:c * wide + (v + 1) * LANES] = (
                (tn * cos + rot * sin) * dh ** -0.5).astype(BF16)

    kt = _dot_nt(wkt_ref[...], xn)
    cost = cost_ref[...]
    sint = sint_ref[...]
    knt = knt_ref[...]
    for g in range(kt.shape[0] // dh):
        a = kt[g * dh:(g + 1) * dh]
        an = a * lax.rsqrt(jnp.sum(a * a, axis=0, keepdims=True) * (1.0 / dh) + EPS) * knt
        x1, x2 = an[:dh // 2], an[dh // 2:]
        o = jnp.concatenate([x1 * cost - x2 * sint, x2 * cost + x1 * sint], axis=0).astype(BF16)
        kt_ref[2 * g * dh:(2 * g + 1) * dh] = o
        kt_ref[(2 * g + 1) * dh:(2 * g + 2) * dh] = o

    v_ref[...] = _dot(xn, wv_ref[...]).astype(BF16)


def _attn_core_kernel(sink_ref, q_ref, ktp_ref, kt_ref, ktq_ref, vp_ref, v_ref, vq_ref, x_ref, wout_ref,
                      o_ref, kbuf_ref, vbuf_ref, obuf_ref, *, blocks_per_seq, n_kv, group):
    blk = ATTN_BLOCK
    dh = ATTN_HEAD_DIM
    tq = q_ref.shape[0]
    n_sub = tq // blk
    j = lax.rem(pl.program_id(0), blocks_per_seq)
    kbuf_ref[:, 0:blk] = ktp_ref[...]
    kbuf_ref[:, blk:blk + tq] = kt_ref[...]
    kbuf_ref[:, blk + tq:] = ktq_ref[...]
    vbuf_ref[0:blk, :] = vp_ref[...]
    vbuf_ref[blk:blk + tq, :] = v_ref[...]
    vbuf_ref[blk + tq:, :] = vq_ref[...]

    qi = lax.broadcasted_iota(jnp.int32, (blk, 3 * blk), 0)
    si = lax.broadcasted_iota(jnp.int32, (blk, 3 * blk), 1)
    rel = si - blk - qi
    band = (rel <= blk) & (rel >= -blk)
    lane_lo = lax.broadcasted_iota(jnp.int32, (blk, LANES), 1) < dh

    for i in range(n_sub):
        r0 = i * blk
        valid = band
        if i == 0:
            valid = valid & jnp.logical_not((j == 0) & (si < blk))
        if i == n_sub - 1:
            valid = valid & jnp.logical_not((j == blocks_per_seq - 1) & (si >= 2 * blk))
        for g in range(n_kv):
            kk = kbuf_ref[g * LANES:(g + 1) * LANES, r0:r0 + 3 * blk]
            vv = vbuf_ref[r0:r0 + 3 * blk, g * LANES:(g + 1) * LANES]
            qs = []
            for v in range(group // 2):
                qp = q_ref[r0:r0 + blk, (g * group + 2 * v) * dh:(g * group + 2 * v + 2) * dh]
                zero = jnp.zeros_like(qp)
                qs += [jnp.where(lane_lo, qp, zero), jnp.where(lane_lo, zero, qp)]
            s4 = _dot(jnp.concatenate(qs, axis=0), kk)
            ps, invs = [], []
            for e in range(group):
                s = jnp.where(valid, s4[e * blk:(e + 1) * blk], -jnp.inf)
                sink = sink_ref[g * group + e]
                m = jnp.maximum(jnp.max(s, axis=-1, keepdims=True), sink)
                pr = jnp.exp(s - m)
                invs.append(1.0 / (jnp.sum(pr, axis=-1, keepdims=True) + jnp.exp(sink - m)))
                ps.append(pr.astype(BF16))
            o4 = _dot(jnp.concatenate(ps, axis=0), vv)
            for v in range(group // 2):
                oa = o4[2 * v * blk:(2 * v + 1) * blk] * invs[2 * v]
                ob = o4[(2 * v + 1) * blk:(2 * v + 2) * blk] * invs[2 * v + 1]
                obuf_ref[r0:r0 + blk, (g * group + 2 * v) * dh:(g * group + 2 * v + 2) * dh] = (
                    jnp.where(lane_lo, oa, ob).astype(BF16))
    o_ref[...] = x_ref[...] + _dot(obuf_ref[...], wout_ref[...])


def _attention(x, nw, wq, wkt, wv2, qn, knt, tables, sink, wout, *, seq, n_kv):
    n, d = x.shape
    tm = min(ROW_TILE, seq)
    bps = seq // tm
    dq, dk2 = wq.shape[1], wv2.shape[1]
    cos, sin, cost, sint = tables
    row = lambda w: pl.BlockSpec((tm, w), lambda i: (i, 0))
    col = pl.BlockSpec((dk2, tm), lambda i: (0, i))
    pos = pl.BlockSpec((tm, LANES), lambda i: (lax.rem(i, bps), 0))
    post = pl.BlockSpec((ATTN_HEAD_DIM // 2, tm), lambda i: (0, lax.rem(i, bps)))
    q, kt, v = pl.pallas_call(
        _attn_qkv_kernel,
        grid=(n // tm,),
        in_specs=[row(d), _resident((1, d)), _resident(wq.shape), _resident(wkt.shape), _resident(wv2.shape),
                  _resident(qn.shape), _resident(knt.shape), pos, pos, post, post],
        out_specs=[row(dq), col, row(dk2)],
        out_shape=[jax.ShapeDtypeStruct((n, dq), BF16), jax.ShapeDtypeStruct((dk2, n), BF16),
                   jax.ShapeDtypeStruct((n, dk2), BF16)],
        compiler_params=_params("arbitrary"),
        name="attn_qkv",
    )(x, nw, wq, wkt, wv2, qn, knt, cos, sin, cost, sint)

    sb = tm // ATTN_BLOCK
    last = n // ATTN_BLOCK - 1
    before = lambda i: jnp.maximum(i * sb - 1, 0)
    after = lambda i: jnp.minimum((i + 1) * sb, last)
    return pl.pallas_call(
        functools.partial(_attn_core_kernel, blocks_per_seq=bps, n_kv=n_kv,
                          group=dq // ATTN_HEAD_DIM // n_kv),
        grid=(n // tm,),
        in_specs=[pl.BlockSpec(memory_space=pltpu.SMEM), row(dq),
                  pl.BlockSpec((dk2, ATTN_BLOCK), lambda i: (0, before(i))), col,
                  pl.BlockSpec((dk2, ATTN_BLOCK), lambda i: (0, after(i))),
                  pl.BlockSpec((ATTN_BLOCK, dk2), lambda i: (before(i), 0)), row(dk2),
                  pl.BlockSpec((ATTN_BLOCK, dk2), lambda i: (after(i), 0)),
                  row(d), _resident(wout.shape)],
        out_specs=row(d),
        out_shape=jax.ShapeDtypeStruct((n, d), F32),
        scratch_shapes=[pltpu.VMEM((dk2, tm + 2 * ATTN_BLOCK), BF16), pltpu.VMEM((tm + 2 * ATTN_BLOCK, dk2), BF16),
                        pltpu.VMEM((tm, dq), BF16)],
        compiler_params=_params("arbitrary"),
        name="attn_core",
    )(sink, q, kt, kt, kt, v, v, v, x, wout)


def _rope_tables(seq):
    half = ATTN_HEAD_DIM // 2
    inv_freq = ROPE_THETA ** (-jnp.arange(half, dtype=F32) / half)
    ang = jnp.arange(seq, dtype=jnp.int32).astype(F32)[:, None] * inv_freq[None, :]
    cos, sin = jnp.cos(ang), jnp.sin(ang)
    return (jnp.tile(cos, (1, 2 * LANES // ATTN_HEAD_DIM)),
            jnp.tile(jnp.concatenate([-sin, sin], axis=1), (1, LANES // ATTN_HEAD_DIM)), cos.T, sin.T)


def kernel(x, norm_w, ffn_w_gate, ffn_w_up, ffn_w_down, ssd_w_in, ssd_conv_w, ssd_conv_b, ssd_dt_bias,
           ssd_a_log, ssd_d, ssd_norm_w, ssd_w_out, attn_w_qkv, attn_q_norm, attn_k_norm, attn_sink,
           attn_w_out):
    batch, seq, d = x.shape
    depth = norm_w.shape[0]
    n_ssd_heads = ssd_d.shape[1]
    d_inner = n_ssd_heads * SSD_HEAD_DIM
    conv_ch = ssd_conv_w.shape[2]
    n_groups = (conv_ch - d_inner) // (2 * SSD_STATE)
    n_heads = attn_sink.shape[1]
    dq = n_heads * ATTN_HEAD_DIM
    dk = (attn_w_qkv.shape[2] - dq) // 2
    n_kv = dk // ATTN_HEAD_DIM
    bf = lambda w: w.astype(BF16)
    pad_lanes = lambda v: jnp.pad(v.astype(F32).reshape(1, -1), ((0, 0), (0, LANES - v.size)))
    tables = _rope_tables(seq)

    h = x.reshape(batch * seq, d)
    for i in range(depth):
        h = _ffn(h, norm_w[i, 0][None], bf(ffn_w_gate[i, 0]), bf(ffn_w_up[i, 0]), bf(ffn_w_down[i, 0]))
        nw = norm_w[i, 1][None]
        j = i // 2
        if i % 2 == 0:
            w_in = ssd_w_in[j]
            wdt = jnp.pad(w_in[:, d_inner + conv_ch:], ((0, 0), (0, LANES - 2 * n_ssd_heads)))
            z, xbc, dt = _ssd_in(h, nw, bf(w_in[:, :d_inner]), bf(w_in[:, d_inner:d_inner + conv_ch]), bf(wdt),
                                 ssd_conv_w[j], ssd_conv_b[j][None], pad_lanes(ssd_dt_bias[j]), seq=seq)
            h = _ssd_scan(h, xbc, dt, z, pad_lanes(ssd_a_log[j]),
                          jnp.repeat(ssd_d[j], SSD_HEAD_DIM)[None], ssd_norm_w[j][None], bf(ssd_w_out[j]),
                          batch=batch, seq=seq, n_heads=n_ssd_heads, n_groups=n_groups)
        else:
            w = attn_w_qkv[j]
            wv = w[:, dq + dk:].reshape(d, n_kv, ATTN_HEAD_DIM)
            wv2 = jnp.concatenate([wv, wv], axis=2).reshape(d, 2 * dk)
            h = _attention(h, nw, bf(w[:, :dq]), bf(w[:, dq:dq + dk].T), bf(wv2),
                           jnp.tile(attn_q_norm[j], LANES // ATTN_HEAD_DIM)[None], attn_k_norm[j][:, None],
                           tables, attn_sink[j], bf(attn_w_out[j]), seq=seq, n_kv=n_kv)
        h = _ffn(h, norm_w[i, 2][None], bf(ffn_w_gate[i, 1]), bf(ffn_w_up[i, 1]), bf(ffn_w_down[i, 1]))
    return h.reshape(batch, seq, d)
```

```python
import functools

import jax
import jax.numpy as jnp
from jax import lax
from jax.experimental import pallas as pl
from jax.experimental.pallas import tpu as pltpu

F32 = jnp.float32
BF16 = jnp.bfloat16
EPS = 1e-6
ROPE_THETA = 10000.0

V7X_VMEM_LIMIT_BYTES = 56 * 1024 * 1024
LANES = 128
HALO = 8

SSD_CHUNK = 128
SSD_HEAD_DIM = 64
SSD_STATE = 128
SSD_GROUP_HEADS = 4
SSD_CONV = 5
ATTN_HEAD_DIM = 64
ATTN_BLOCK = 128

ROW_TILE = 512
MXU_WIDTH = 256
CONV_PHASES = 4
LOG2E = 1.4426950408889634


def _params(*semantics):
    return pltpu.CompilerParams(dimension_semantics=semantics, vmem_limit_bytes=V7X_VMEM_LIMIT_BYTES)


def _resident(shape):
    return pl.BlockSpec(shape, lambda *_: (0,) * len(shape), pipeline_mode=pl.Buffered(1))


def _rmsnorm(x, w):
    return x * lax.rsqrt(jnp.mean(x * x, axis=-1, keepdims=True) + EPS) * w


def _silu(x):
    h = 0.5 * x
    return h + h * jnp.tanh(h)


def _dot(a, b):
    return jnp.dot(a, b, preferred_element_type=F32)


def _dot_nt(a, b):
    return lax.dot_general(a, b, (((1,), (1,)), ((), ())), preferred_element_type=F32)


def _ffn_kernel(x_ref, nw_ref, wg_ref, wu_ref, wd_ref, o_ref, h_ref, *, ff_chunk):
    x = x_ref[...]
    xn = _rmsnorm(x, nw_ref[...]).astype(BF16)
    for c in range(wg_ref.shape[1] // ff_chunk):
        sl = slice(c * ff_chunk, (c + 1) * ff_chunk)
        g = _dot(xn, wg_ref[:, sl])
        u = _dot(xn, wu_ref[:, sl])
        h_ref[:, sl] = (_silu(g) * u).astype(BF16)
    o_ref[...] = x + 0.5 * _dot(h_ref[...], wd_ref[...])


def _ffn(x, nw, wg, wu, wd):
    n, d = x.shape
    d_ff = wg.shape[1]
    tm = min(ROW_TILE, n)
    row = pl.BlockSpec((tm, d), lambda i: (i, 0))
    return pl.pallas_call(
        functools.partial(_ffn_kernel, ff_chunk=MXU_WIDTH),
        grid=(n // tm,),
        in_specs=[row, _resident((1, d)), _resident((d, d_ff)), _resident((d, d_ff)), _resident((d_ff, d))],
        out_specs=row,
        out_shape=jax.ShapeDtypeStruct((n, d), F32),
        scratch_shapes=[pltpu.VMEM((tm, d_ff), BF16)],
        compiler_params=_params("arbitrary"),
        name="ffn",
    )(x, nw, wg, wu, wd)


def _ssd_in_kernel(xp_ref, x_ref, xq_ref, nw_ref, wz_ref, wxbc_ref, wdt_ref, cw_ref, cb_ref, dtb_ref,
                   z_ref, xbc_ref, dt_ref, p_ref, a_ref, *, blocks_per_seq, col_chunk):
    tm = x_ref.shape[0]
    j = lax.rem(pl.program_id(0), blocks_per_seq)
    nw = nw_ref[...]
    xm = _rmsnorm(x_ref[...], nw)
    xp = _rmsnorm(xp_ref[...], nw) * (j != 0).astype(F32)
    xq = _rmsnorm(xq_ref[...], nw) * (j != blocks_per_seq - 1).astype(F32)
    xm_bf = xm.astype(BF16)
    xe_bf = jnp.concatenate([xp, xm, xq], axis=0).astype(BF16)

    dt_ref[...] = jax.nn.softplus(_dot(xm_bf, wdt_ref[...]) + dtb_ref[...])
    for c in range(wz_ref.shape[1] // col_chunk):
        sl = slice(c * col_chunk, (c + 1) * col_chunk)
        z_ref[:, sl] = _dot(xm_bf, wz_ref[:, sl]).astype(BF16)
    slabs = col_chunk // LANES
    n_chunks = wxbc_ref.shape[1] // col_chunk

    def project(c):
        u = _dot(xe_bf, wxbc_ref[:, c * col_chunk:(c + 1) * col_chunk])
        for s in range(slabs):
            p_ref[c % 2, s] = u[:, s * LANES:(s + 1) * LANES]

    def conv(c):
        for s in range(slabs):
            lanes = slice(c * col_chunk + s * LANES, c * col_chunk + (s + 1) * LANES)
            for ph in range(CONV_PHASES):
                a = cb_ref[:, lanes]
                for k in range(SSD_CONV):
                    a = a + cw_ref[k:k + 1, lanes] * p_ref[
                        c % 2, s, pl.ds(ph + HALO - SSD_CONV // 2 + k, tm // CONV_PHASES, stride=CONV_PHASES), :]
                a_ref[s, pl.ds(ph, tm // CONV_PHASES, stride=CONV_PHASES), :] = _silu(a)
            xbc_ref[:, lanes] = a_ref[s].astype(BF16)

    project(0)
    for c in range(n_chunks):
        if c + 1 < n_chunks:
            project(c + 1)
        conv(c)


def _ssd_in(x, nw, wz, wxbc, wdt, cw, cb, dtb, *, seq):
    n, d = x.shape
    tm = min(ROW_TILE, seq)
    bps = seq // tm
    hb = tm // HALO
    last_halo = n // HALO - 1
    row = lambda w: pl.BlockSpec((tm, w), lambda i: (i, 0))
    prev = pl.BlockSpec((HALO, d), lambda i: (jnp.maximum(i * hb - 1, 0), 0))
    nxt = pl.BlockSpec((HALO, d), lambda i: (jnp.minimum((i + 1) * hb, last_halo), 0))
    col_chunk = 2 * MXU_WIDTH
    return pl.pallas_call(
        functools.partial(_ssd_in_kernel, blocks_per_seq=bps, col_chunk=col_chunk),
        grid=(n // tm,),
        in_specs=[prev, row(d), nxt, _resident((1, d)), _resident(wz.shape), _resident(wxbc.shape),
                  _resident(wdt.shape), _resident(cw.shape), _resident(cb.shape), _resident(dtb.shape)],
        out_specs=[row(wz.shape[1]), row(wxbc.shape[1]), row(LANES)],
        out_shape=[jax.ShapeDtypeStruct((n, wz.shape[1]), BF16),
                   jax.ShapeDtypeStruct((n, wxbc.shape[1]), BF16),
                   jax.ShapeDtypeStruct((n, LANES), F32)],
        scratch_shapes=[pltpu.VMEM((2, col_chunk // LANES, tm + 2 * HALO, LANES), F32),
                        pltpu.VMEM((col_chunk // LANES, tm, LANES), F32)],
        compiler_params=_params("arbitrary"),
        name="ssd_in",
    )(x, x, x, nw, wz, wxbc, wdt, cw, cb, dtb)


def _chunk_decays(dt, a_row, tril, n_heads):
    t = dt.shape[0]
    da = dt * a_row
    prefix = jnp.dot(tril, da, precision=lax.Precision.HIGHEST, preferred_element_type=F32)
    total = prefix[t - 1:t, :]
    lane = lax.broadcasted_iota(jnp.int32, dt.shape, 1)
    return jnp.where(lane < n_heads, prefix, total - prefix + da), total


def _bcol(v_t, j):
    return jnp.broadcast_to(v_t[j:j + 1, :], (LANES, v_t.shape[1])).T


def _head_lanes(cols):
    first = lax.broadcasted_iota(jnp.int32, cols[0].shape, 1) < SSD_HEAD_DIM
    return jnp.concatenate([jnp.where(first, cols[0], cols[1]), jnp.where(first, cols[2], cols[3])], axis=1)


def _block_diag(xg):
    lane = lax.broadcasted_iota(jnp.int32, xg.shape, 1)
    zero = jnp.zeros_like(xg)
    return jnp.concatenate(
        [jnp.where((lane >= e * SSD_HEAD_DIM) & (lane < (e + 1) * SSD_HEAD_DIM), xg, zero)
         for e in range(SSD_GROUP_HEADS)], axis=0)


def _ssd_bwd_kernel(xbc_ref, dt_ref, alog_ref, yb_ref, st_ref, *, n_heads, n_groups):
    t = SSD_CHUNK
    gh = SSD_GROUP_HEADS
    gw = gh * SSD_HEAD_DIM
    d_inner = n_heads * SSD_HEAD_DIM
    n_chunks = xbc_ref.shape[0] // t

    @pl.when(pl.program_id(1) == 0)
    def _():
        st_ref[...] = jnp.zeros_like(st_ref)

    a_row = -jnp.exp(alog_ref[...])
    ri = lax.broadcasted_iota(jnp.int32, (t, t), 0)
    ci = lax.broadcasted_iota(jnp.int32, (t, t), 1)
    tril = (ri >= ci).astype(F32)

    def chunk(i, carry):
        r0 = pl.multiple_of((n_chunks - 1 - i) * t, t)
        rows = pl.ds(r0, t)
        dt = dt_ref[rows, :]
        cum, total = _chunk_decays(dt, a_row, tril, n_heads)
        ws_t = (dt * jnp.exp(total - cum)).T
        decay = jnp.exp(total)
        for g in range(n_groups):
            h0 = n_heads + g * gh
            xg = xbc_ref[rows, g * gw:(g + 1) * gw]
            bg = xbc_ref[rows, d_inner + g * SSD_STATE:d_inner + (g + 1) * SSD_STATE]
            cg = xbc_ref[rows, d_inner + (n_groups + g) * SSD_STATE:d_inner + (n_groups + g + 1) * SSD_STATE]
            st = st_ref[g]
            yb_ref[rows, g * gw:(g + 1) * gw] = _dot(cg, st.astype(BF16))
            bt = bg.astype(F32).T
            bws = [(bt * ws_t[h0 + e:h0 + e + 1, :]).astype(BF16) for e in range(gh)]
            decay_g = _head_lanes([jnp.broadcast_to(decay[:, h0 + e:h0 + e + 1], (1, LANES)) for e in range(gh)])
            st_ref[g] = st * decay_g + _dot(jnp.concatenate(bws, axis=1), _block_diag(xg))
        return carry

    lax.fori_loop(0, n_chunks, chunk, 0)


def _ssd_main_kernel(xbc_ref, dt_ref, z_ref, yb_ref, x_ref, alog_ref, dexp_ref, nw_ref, wout_ref,
                     o_ref, st_ref, y_ref, *, n_heads, n_groups):
    t = SSD_CHUNK
    gh = SSD_GROUP_HEADS
    gw = gh * SSD_HEAD_DIM
    d_inner = n_heads * SSD_HEAD_DIM
    n_chunks = xbc_ref.shape[0] // t

    @pl.when(pl.program_id(1) == 0)
    def _():
        st_ref[...] = jnp.zeros_like(st_ref)

    a_row = -jnp.exp(alog_ref[...])
    ri = lax.broadcasted_iota(jnp.int32, (t, t), 0)
    ci = lax.broadcasted_iota(jnp.int32, (t, t), 1)
    lower = ri >= ci
    tril = lower.astype(F32)

    def chunk(i, carry):
        r0 = pl.multiple_of(i * t, t)
        rows = pl.ds(r0, t)
        dt = dt_ref[rows, :]
        cum, total = _chunk_decays(dt, a_row, tril, n_heads)
        cum_t = (cum * LOG2E).T
        dt_t = dt.T
        ws_t = (dt * jnp.exp(total - cum)).T
        for g in range(n_groups):
            cols = slice(g * gw, (g + 1) * gw)
            xg = xbc_ref[rows, cols]
            bg = xbc_ref[rows, d_inner + g * SSD_STATE:d_inner + (g + 1) * SSD_STATE]
            cg = xbc_ref[rows, d_inner + (n_groups + g) * SSD_STATE:d_inner + (n_groups + g + 1) * SSD_STATE]
            bt = bg.astype(F32).T
            cb = _dot(cg, bt.astype(BF16)).astype(BF16)
            ms, bws, ef, eb = [], [], [], []
            for e in range(gh):
                hf = g * gh + e
                hb = n_heads + hf
                cf_col = _bcol(cum_t, hf)
                cb_col = _bcol(cum_t, hb)
                arg = jnp.where(lower, cf_col - cum_t[hf:hf + 1, :], cb_col - cum_t[hb:hb + 1, :])
                dtf = dt_t[hf:hf + 1, :]
                dtb = dt_t[hb:hb + 1, :]
                w = jnp.where(ri > ci, dtf, jnp.where(ri < ci, dtb, dtf + dtb))
                ms.append((jnp.exp2(arg) * w).astype(BF16) * cb)
                bws.append((bt * ws_t[hf:hf + 1, :]).astype(BF16))
                ef.append(jnp.exp2(cf_col))
                eb.append(jnp.exp2(cb_col))
            lhs = jnp.concatenate([jnp.concatenate(ms, axis=1), jnp.concatenate(bws, axis=1)], axis=0)
            r = _dot(lhs, _block_diag(xg))
            e_f = _head_lanes(ef)
            st = st_ref[g]
            y_ref[rows, cols] = (r[:t] + xg.astype(F32) * dexp_ref[:, cols] + _dot(cg, st.astype(BF16)) * e_f
                                 + yb_ref[rows, cols] * _head_lanes(eb))
            st_ref[g] = st * e_f[t - 1:t, :] + r[t:]
        return carry

    lax.fori_loop(0, n_chunks, chunk, 0)

    y = y_ref[...] * _silu(z_ref[...].astype(F32))
    o_ref[...] = x_ref[...] + _dot(_rmsnorm(y, nw_ref[...]).astype(BF16), wout_ref[...])


def _ssd_scan(x, xbc, dt, z, alog, dexp, nw, wout, *, batch, seq, n_heads, n_groups):
    n, d = x.shape
    d_inner = n_heads * SSD_HEAD_DIM
    tb = min(ROW_TILE, seq)
    bps = seq // tb
    state = pltpu.VMEM((n_groups, SSD_STATE, SSD_GROUP_HEADS * SSD_HEAD_DIM), F32)
    rev = lambda w: pl.BlockSpec((tb, w), lambda b, j: (b * bps + bps - 1 - j, 0))
    fwd = lambda w: pl.BlockSpec((tb, w), lambda b, j: (b * bps + j, 0))

    yb = pl.pallas_call(
        functools.partial(_ssd_bwd_kernel, n_heads=n_heads, n_groups=n_groups),
        grid=(batch, bps),
        in_specs=[rev(xbc.shape[1]), rev(LANES), _resident(alog.shape)],
        out_specs=rev(d_inner),
        out_shape=jax.ShapeDtypeStruct((n, d_inner), F32),
        scratch_shapes=[state],
        compiler_params=_params("arbitrary", "arbitrary"),
        name="ssd_bwd",
    )(xbc, dt, alog)

    return pl.pallas_call(
        functools.partial(_ssd_main_kernel, n_heads=n_heads, n_groups=n_groups),
        grid=(batch, bps),
        in_specs=[fwd(xbc.shape[1]), fwd(LANES), fwd(d_inner), fwd(d_inner), fwd(d),
                  _resident(alog.shape), _resident(dexp.shape), _resident(nw.shape), _resident(wout.shape)],
        out_specs=fwd(d),
        out_shape=jax.ShapeDtypeStruct((n, d), F32),
        scratch_shapes=[state, pltpu.VMEM((tb, d_inner), F32)],
        compiler_params=_params("arbitrary", "arbitrary"),
        name="ssd_main",
    )(xbc, dt, z, yb, x, alog, dexp, nw, wout)


def _attn_qkv_kernel(x_ref, nw_ref, wt_ref, qnt_ref, knt_ref, cost_ref, sint_ref, qt_ref, k_ref, vt_ref):
    dh = ATTN_HEAD_DIM
    dq = qt_ref.shape[0]
    dk = vt_ref.shape[0]
    xn = _rmsnorm(x_ref[...], nw_ref[...]).astype(BF16)
    cost = cost_ref[...]
    sint = sint_ref[...]

    def norm_rope(a, w_col, scale):
        an = a * lax.rsqrt(jnp.sum(a * a, axis=0, keepdims=True) * (1.0 / dh) + EPS) * w_col
        x1, x2 = an[:dh // 2], an[dh // 2:]
        return jnp.concatenate([x1 * cost - x2 * sint, x2 * cost + x1 * sint], axis=0) * scale

    q_scale = dh ** -0.5 * LOG2E
    qnt = qnt_ref[...]
    for c in range(dq // MXU_WIDTH):
        y = _dot_nt(wt_ref[c * MXU_WIDTH:(c + 1) * MXU_WIDTH, :], xn)
        for hh in range(MXU_WIDTH // dh):
            qt_ref[c * MXU_WIDTH + hh * dh:c * MXU_WIDTH + (hh + 1) * dh, :] = norm_rope(
                y[hh * dh:(hh + 1) * dh], qnt, q_scale).astype(BF16)
    yk = _dot_nt(wt_ref[dq:dq + dk, :], xn)
    knt = knt_ref[...]
    kt = jnp.concatenate([norm_rope(yk[g * dh:(g + 1) * dh], knt, 1.0) for g in range(dk // dh)], axis=0)
    k_ref[...] = kt.T.astype(BF16)
    vt_ref[...] = _dot_nt(wt_ref[dq + dk:, :], xn).astype(BF16)


def _attn_core_kernel(sink_ref, qt_ref, kp_ref, k_ref, kq_ref, vtp_ref, vt_ref, vtq_ref, x_ref, woutt_ref,
                      o_ref, kbuf_ref, vbuf_ref, obuf_ref, s_ref, p_ref, inv_ref, *, blocks_per_seq, n_kv, group):
    blk = ATTN_BLOCK
    dh = ATTN_HEAD_DIM
    tq = qt_ref.shape[1]
    n_sub = tq // blk
    j = lax.rem(pl.program_id(0), blocks_per_seq)
    kbuf_ref[0:blk, :] = kp_ref[...]
    kbuf_ref[blk:blk + tq, :] = k_ref[...]
    kbuf_ref[blk + tq:, :] = kq_ref[...]
    vbuf_ref[:, 0:blk] = vtp_ref[...]
    vbuf_ref[:, blk:blk + tq] = vt_ref[...]
    vbuf_ref[:, blk + tq:] = vtq_ref[...]

    si = lax.broadcasted_iota(jnp.int32, (3 * blk, blk), 0)
    qi = lax.broadcasted_iota(jnp.int32, (3 * blk, blk), 1)
    rel = si - blk - qi
    band = (rel <= blk) & (rel >= -blk)

    def bias_of(i):
        valid = band
        if i == 0:
            valid = valid & jnp.logical_not((j == 0) & (si < blk))
        if i == n_sub - 1:
            valid = valid & jnp.logical_not((j == blocks_per_seq - 1) & (si >= 2 * blk))
        return jnp.tile(jnp.where(valid, 0.0, -jnp.inf), (1, group))

    def scores(i, g, slot):
        r0 = i * blk
        kk = kbuf_ref[r0:r0 + 3 * blk, (g // 2) * LANES:(g // 2 + 1) * LANES]
        qcat = jnp.concatenate(
            [qt_ref[(g * group + e) * dh:(g * group + e + 1) * dh, r0:r0 + blk] for e in range(group)], axis=1)
        zero = jnp.zeros_like(qcat)
        s_ref[slot] = _dot(kk, jnp.concatenate([qcat, zero] if g % 2 == 0 else [zero, qcat], axis=0))

    def softmax(bias, g, slot):
        s = s_ref[slot] + bias
        sink = jnp.concatenate(
            [jnp.full((1, blk), sink_ref[g * group + e] * LOG2E, F32) for e in range(group)], axis=1)
        m = jnp.maximum(jnp.max(s, axis=0, keepdims=True), sink)
        pr = jnp.exp2(s - m)
        inv_ref[slot] = 1.0 / (jnp.sum(pr, axis=0, keepdims=True) + jnp.exp2(sink - m))
        p_ref[slot] = pr.astype(BF16)

    def values(i, g, slot):
        r0 = i * blk
        ot = _dot(vbuf_ref[g * dh:(g + 1) * dh, r0:r0 + 3 * blk], p_ref[slot]) * inv_ref[slot]
        for e in range(group):
            obuf_ref[(g * group + e) * dh:(g * group + e + 1) * dh, r0:r0 + blk] = (
                ot[:, e * blk:(e + 1) * blk].astype(BF16))

    items = [(i, g) for i in range(n_sub) for g in range(n_kv)]
    biases = [bias_of(i) for i in range(n_sub)]
    scores(*items[0], 0)
    for n, (i, g) in enumerate(items):
        if n + 1 < len(items):
            scores(*items[n + 1], (n + 1) % 2)
        softmax(biases[i], g, n % 2)
        if n > 0:
            values(*items[n - 1], (n - 1) % 2)
    values(*items[-1], (len(items) - 1) % 2)
    o_ref[...] = x_ref[...] + _dot(woutt_ref[...], obuf_ref[...]).T


def _attention(x, nw, wt, qnt, knt, tables, sink, woutt, *, seq, n_kv):
    n, d = x.shape
    tm = min(ROW_TILE, seq)
    bps = seq // tm
    n_heads = sink.shape[0]
    dq = n_heads * ATTN_HEAD_DIM
    dk = n_kv * ATTN_HEAD_DIM
    assert n_kv % 2 == 0 and wt.shape[0] == dq + 2 * dk
    cost, sint = tables
    row = lambda w: pl.BlockSpec((tm, w), lambda i: (i, 0))
    col = lambda h: pl.BlockSpec((h, tm), lambda i: (0, i))
    post = pl.BlockSpec((ATTN_HEAD_DIM // 2, tm), lambda i: (0, lax.rem(i, bps)))
    qt, k, vt = pl.pallas_call(
        _attn_qkv_kernel,
        grid=(n // tm,),
        in_specs=[row(d), _resident((1, d)), _resident(wt.shape), _resident(qnt.shape), _resident(knt.shape),
                  post, post],
        out_specs=[col(dq), row(dk), col(dk)],
        out_shape=[jax.ShapeDtypeStruct((dq, n), BF16), jax.ShapeDtypeStruct((n, dk), BF16),
                   jax.ShapeDtypeStruct((dk, n), BF16)],
        compiler_params=_params("arbitrary"),
        name="attn_qkv",
    )(x, nw, wt, qnt, knt, cost, sint)

    sb = tm // ATTN_BLOCK
    last = n // ATTN_BLOCK - 1
    before = lambda i: jnp.maximum(i * sb - 1, 0)
    after = lambda i: jnp.minimum((i + 1) * sb, last)
    return pl.pallas_call(
        functools.partial(_attn_core_kernel, blocks_per_seq=bps, n_kv=n_kv, group=n_heads // n_kv),
        grid=(n // tm,),
        in_specs=[pl.BlockSpec(memory_space=pltpu.SMEM), col(dq),
                  pl.BlockSpec((ATTN_BLOCK, dk), lambda i: (before(i), 0)), row(dk),
                  pl.BlockSpec((ATTN_BLOCK, dk), lambda i: (after(i), 0)),
                  pl.BlockSpec((dk, ATTN_BLOCK), lambda i: (0, before(i))), col(dk),
                  pl.BlockSpec((dk, ATTN_BLOCK), lambda i: (0, after(i))),
                  row(d), _resident(woutt.shape)],
        out_specs=row(d),
        out_shape=jax.ShapeDtypeStruct((n, d), F32),
        scratch_shapes=[pltpu.VMEM((tm + 2 * ATTN_BLOCK, dk), BF16), pltpu.VMEM((dk, tm + 2 * ATTN_BLOCK), BF16),
                        pltpu.VMEM((dq, tm), BF16),
                        pltpu.VMEM((2, 3 * ATTN_BLOCK, n_heads // n_kv * ATTN_BLOCK), F32),
                        pltpu.VMEM((2, 3 * ATTN_BLOCK, n_heads // n_kv * ATTN_BLOCK), BF16),
                        pltpu.VMEM((2, 1, n_heads // n_kv * ATTN_BLOCK), F32)],
        compiler_params=_params("arbitrary"),
        name="attn_core",
    )(sink, qt, k, k, k, vt, vt, vt, x, woutt)


def _rope_tables(seq):
    half = ATTN_HEAD_DIM // 2
    inv_freq = ROPE_THETA ** (-jnp.arange(half, dtype=F32) / half)
    ang = inv_freq[:, None] * jnp.arange(seq, dtype=jnp.int32).astype(F32)[None, :]
    return jnp.cos(ang), jnp.sin(ang)


def kernel(x, norm_w, ffn_w_gate, ffn_w_up, ffn_w_down, ssd_w_in, ssd_conv_w, ssd_conv_b, ssd_dt_bias,
           ssd_a_log, ssd_d, ssd_norm_w, ssd_w_out, attn_w_qkv, attn_q_norm, attn_k_norm, attn_sink,
           attn_w_out):
    batch, seq, d = x.shape
    depth = norm_w.shape[0]
    n_ssd_heads = ssd_d.shape[1]
    d_inner = n_ssd_heads * SSD_HEAD_DIM
    conv_ch = ssd_conv_w.shape[2]
    n_groups = (conv_ch - d_inner) // (2 * SSD_STATE)
    n_kv = (attn_w_qkv.shape[2] // ATTN_HEAD_DIM - attn_sink.shape[1]) // 2
    bf = lambda w: w.astype(BF16)
    pad_lanes = lambda v: jnp.pad(v.astype(F32).reshape(1, -1), ((0, 0), (0, LANES - v.size)))
    tables = _rope_tables(seq)

    h = x.reshape(batch * seq, d)
    for i in range(depth):
        h = _ffn(h, norm_w[i, 0][None], bf(ffn_w_gate[i, 0]), bf(ffn_w_up[i, 0]), bf(ffn_w_down[i, 0]))
        nw = norm_w[i, 1][None]
        j = i // 2
        if i % 2 == 0:
            w_in = ssd_w_in[j]
            wdt = jnp.pad(w_in[:, d_inner + conv_ch:], ((0, 0), (0, LANES - 2 * n_ssd_heads)))
            z, xbc, dt = _ssd_in(h, nw, bf(w_in[:, :d_inner]), bf(w_in[:, d_inner:d_inner + conv_ch]), bf(wdt),
                                 ssd_conv_w[j], ssd_conv_b[j][None], pad_lanes(ssd_dt_bias[j]), seq=seq)
            h = _ssd_scan(h, xbc, dt, z, pad_lanes(ssd_a_log[j]),
                          jnp.repeat(ssd_d[j], SSD_HEAD_DIM)[None], ssd_norm_w[j][None], bf(ssd_w_out[j]),
                          batch=batch, seq=seq, n_heads=n_ssd_heads, n_groups=n_groups)
        else:
            h = _attention(h, nw, bf(attn_w_qkv[j].T), attn_q_norm[j][:, None], attn_k_norm[j][:, None], tables,
                           attn_sink[j], bf(attn_w_out[j].T), seq=seq, n_kv=n_kv)
        h = _ffn(h, norm_w[i, 2][None], bf(ffn_w_gate[i, 1]), bf(ffn_w_up[i, 1]), bf(ffn_w_down[i, 1]))
    return h.reshape(batch, seq, d)
```

```python
import functools

import jax
import jax.numpy as jnp
from jax import lax
from jax.experimental import pallas as pl
from jax.experimental.pallas import tpu as pltpu

F32 = jnp.float32
BF16 = jnp.bfloat16
EPS = 1e-6
ROPE_THETA = 10000.0

V7X_VMEM_LIMIT_BYTES = 56 * 1024 * 1024
LANES = 128
HALO = 8

SSD_CHUNK = 128
SSD_HEAD_DIM = 64
SSD_STATE = 128
SSD_GROUP_HEADS = 4
SSD_CONV = 5
ATTN_HEAD_DIM = 64
ATTN_BLOCK = 128

ROW_TILE = 512
FFN_ROW_TILE = 1024
MXU_WIDTH = 256
CONV_PHASES = 4
LOG2E = 1.4426950408889634


def _params(*semantics):
    return pltpu.CompilerParams(dimension_semantics=semantics, vmem_limit_bytes=V7X_VMEM_LIMIT_BYTES)


def _resident(shape):
    return pl.BlockSpec(shape, lambda *_: (0,) * len(shape), pipeline_mode=pl.Buffered(1))


def _rmsnorm(x, w):
    return x * lax.rsqrt(jnp.mean(x * x, axis=-1, keepdims=True) + EPS) * w


def _silu(x):
    h = 0.5 * x
    return h + h * jnp.tanh(h)


def _dot(a, b):
    return jnp.dot(a, b, preferred_element_type=F32)


def _dot_nt(a, b):
    return lax.dot_general(a, b, (((1,), (1,)), ((), ())), preferred_element_type=F32)


def _ffn_kernel(x_ref, nw_ref, wg_ref, wu_ref, wd_ref, o_ref, h_ref, *, ff_chunk):
    x = x_ref[...]
    xn = _rmsnorm(x, nw_ref[...]).astype(BF16)
    for c in range(wg_ref.shape[1] // ff_chunk):
        sl = slice(c * ff_chunk, (c + 1) * ff_chunk)
        g = _dot(xn, wg_ref[:, sl])
        u = _dot(xn, wu_ref[:, sl])
        h_ref[:, sl] = (_silu(g) * u).astype(BF16)
    o_ref[...] = x + 0.5 * _dot(h_ref[...], wd_ref[...])


def _ffn(x, nw, wg, wu, wd):
    n, d = x.shape
    d_ff = wg.shape[1]
    tm = min(FFN_ROW_TILE, n)
    row = pl.BlockSpec((tm, d), lambda i: (i, 0))
    return pl.pallas_call(
        functools.partial(_ffn_kernel, ff_chunk=MXU_WIDTH),
        grid=(n // tm,),
        in_specs=[row, _resident((1, d)), _resident((d, d_ff)), _resident((d, d_ff)), _resident((d_ff, d))],
        out_specs=row,
        out_shape=jax.ShapeDtypeStruct((n, d), F32),
        scratch_shapes=[pltpu.VMEM((tm, d_ff), BF16)],
        compiler_params=_params("arbitrary"),
        name="ffn",
    )(x, nw, wg, wu, wd)


def _ssd_in_kernel(xp_ref, x_ref, xq_ref, nw_ref, wz_ref, wxbc_ref, wdt_ref, cw_ref, cb_ref, dtb_ref,
                   z_ref, xbc_ref, dt_ref, p_ref, a_ref, *, blocks_per_seq, col_chunk):
    tm = x_ref.shape[0]
    j = lax.rem(pl.program_id(0), blocks_per_seq)
    nw = nw_ref[...]
    xm = _rmsnorm(x_ref[...], nw)
    xp = _rmsnorm(xp_ref[...], nw) * (j != 0).astype(F32)
    xq = _rmsnorm(xq_ref[...], nw) * (j != blocks_per_seq - 1).astype(F32)
    xm_bf = xm.astype(BF16)
    xe_bf = jnp.concatenate([xp, xm, xq], axis=0).astype(BF16)

    slabs = col_chunk // LANES
    n_chunks = wxbc_ref.shape[1] // col_chunk

    def project(c):
        u = _dot(xe_bf, wxbc_ref[:, c * col_chunk:(c + 1) * col_chunk])
        for s in range(slabs):
            p_ref[c % 2, s] = u[:, s * LANES:(s + 1) * LANES]

    def conv(c):
        for s in range(slabs):
            lanes = slice(c * col_chunk + s * LANES, c * col_chunk + (s + 1) * LANES)
            for ph in range(CONV_PHASES):
                a = cb_ref[:, lanes]
                for k in range(SSD_CONV):
                    a = a + cw_ref[k:k + 1, lanes] * p_ref[
                        c % 2, s, pl.ds(ph + HALO - SSD_CONV // 2 + k, tm // CONV_PHASES, stride=CONV_PHASES), :]
                a_ref[s, pl.ds(ph, tm // CONV_PHASES, stride=CONV_PHASES), :] = _silu(a)
            xbc_ref[:, lanes] = a_ref[s].astype(BF16)

    def project_z(c):
        sl = slice(c * col_chunk, (c + 1) * col_chunk)
        z_ref[:, sl] = _dot(xm_bf, wz_ref[:, sl]).astype(BF16)

    z_chunks = list(range(wz_ref.shape[1] // col_chunk))
    project(0)
    for c in range(n_chunks):
        if c + 1 < n_chunks:
            project(c + 1)
        if c % 2 == 1 and z_chunks:
            project_z(z_chunks.pop(0))
        conv(c)
    dt_ref[...] = jax.nn.softplus(_dot(xm_bf, wdt_ref[...]) + dtb_ref[...])
    for c in z_chunks:
        project_z(c)


def _ssd_in(x, nw, wz, wxbc, wdt, cw, cb, dtb, *, seq):
    n, d = x.shape
    tm = min(ROW_TILE, seq)
    bps = seq // tm
    hb = tm // HALO
    last_halo = n // HALO - 1
    row = lambda w: pl.BlockSpec((tm, w), lambda i: (i, 0))
    prev = pl.BlockSpec((HALO, d), lambda i: (jnp.maximum(i * hb - 1, 0), 0))
    nxt = pl.BlockSpec((HALO, d), lambda i: (jnp.minimum((i + 1) * hb, last_halo), 0))
    col_chunk = 2 * MXU_WIDTH
    return pl.pallas_call(
        functools.partial(_ssd_in_kernel, blocks_per_seq=bps, col_chunk=col_chunk),
        grid=(n // tm,),
        in_specs=[prev, row(d), nxt, _resident((1, d)), _resident(wz.shape), _resident(wxbc.shape),
                  _resident(wdt.shape), _resident(cw.shape), _resident(cb.shape), _resident(dtb.shape)],
        out_specs=[row(wz.shape[1]), row(wxbc.shape[1]), row(LANES)],
        out_shape=[jax.ShapeDtypeStruct((n, wz.shape[1]), BF16),
                   jax.ShapeDtypeStruct((n, wxbc.shape[1]), BF16),
                   jax.ShapeDtypeStruct((n, LANES), F32)],
        scratch_shapes=[pltpu.VMEM((2, col_chunk // LANES, tm + 2 * HALO, LANES), F32),
                        pltpu.VMEM((col_chunk // LANES, tm, LANES), F32)],
        compiler_params=_params("arbitrary"),
        name="ssd_in",
    )(x, x, x, nw, wz, wxbc, wdt, cw, cb, dtb)


def _chunk_decays(dt, a_row, tril, n_heads):
    t = dt.shape[0]
    da = dt * a_row
    prefix = jnp.dot(tril, da, precision=lax.Precision.HIGHEST, preferred_element_type=F32)
    total = prefix[t - 1:t, :]
    lane = lax.broadcasted_iota(jnp.int32, dt.shape, 1)
    return jnp.where(lane < n_heads, prefix, total - prefix + da), total


def _bcol(v_t, j):
    return jnp.broadcast_to(v_t[j:j + 1, :], (LANES, v_t.shape[1])).T


def _head_lanes(cols):
    first = lax.broadcasted_iota(jnp.int32, cols[0].shape, 1) < SSD_HEAD_DIM
    return jnp.concatenate([jnp.where(first, cols[0], cols[1]), jnp.where(first, cols[2], cols[3])], axis=1)


def _block_diag(xg):
    lane = lax.broadcasted_iota(jnp.int32, xg.shape, 1)
    zero = jnp.zeros_like(xg)
    return jnp.concatenate(
        [jnp.where((lane >= e * SSD_HEAD_DIM) & (lane < (e + 1) * SSD_HEAD_DIM), xg, zero)
         for e in range(SSD_GROUP_HEADS)], axis=0)


def _ssd_bwd_kernel(xbc_ref, dt_ref, alog_ref, yb_ref, st_ref, *, n_heads, n_groups):
    t = SSD_CHUNK
    gh = SSD_GROUP_HEADS
    gw = gh * SSD_HEAD_DIM
    d_inner = n_heads * SSD_HEAD_DIM
    n_chunks = xbc_ref.shape[0] // t

    @pl.when(pl.program_id(1) == 0)
    def _():
        st_ref[...] = jnp.zeros_like(st_ref)

    a_row = -jnp.exp(alog_ref[...])
    ri = lax.broadcasted_iota(jnp.int32, (t, t), 0)
    ci = lax.broadcasted_iota(jnp.int32, (t, t), 1)
    tril = (ri >= ci).astype(F32)

    def chunk(i, carry):
        r0 = pl.multiple_of((n_chunks - 1 - i) * t, t)
        rows = pl.ds(r0, t)
        dt = dt_ref[rows, :]
        cum, total = _chunk_decays(dt, a_row, tril, n_heads)
        ws_t = (dt * jnp.exp(total - cum)).T
        decay = jnp.exp(total)
        for g in range(n_groups):
            h0 = n_heads + g * gh
            xg = xbc_ref[rows, g * gw:(g + 1) * gw]
            bg = xbc_ref[rows, d_inner + g * SSD_STATE:d_inner + (g + 1) * SSD_STATE]
            cg = xbc_ref[rows, d_inner + (n_groups + g) * SSD_STATE:d_inner + (n_groups + g + 1) * SSD_STATE]
            st = st_ref[g]
            yb_ref[rows, g * gw:(g + 1) * gw] = _dot(cg, st.astype(BF16))
            bt = bg.astype(F32).T
            bws = [(bt * ws_t[h0 + e:h0 + e + 1, :]).astype(BF16) for e in range(gh)]
            decay_g = _head_lanes([jnp.broadcast_to(decay[:, h0 + e:h0 + e + 1], (1, LANES)) for e in range(gh)])
            st_ref[g] = st * decay_g + _dot(jnp.concatenate(bws, axis=1), _block_diag(xg))
        return carry

    lax.fori_loop(0, n_chunks, chunk, 0)


def _ssd_main_kernel(xbc_ref, dt_ref, z_ref, yb_ref, x_ref, alog_ref, dexp_ref, nw_ref, wout_ref,
                     o_ref, st_ref, y_ref, yn_ref, *, n_heads, n_groups):
    t = SSD_CHUNK
    gh = SSD_GROUP_HEADS
    gw = gh * SSD_HEAD_DIM
    d_inner = n_heads * SSD_HEAD_DIM
    n_chunks = xbc_ref.shape[0] // t

    @pl.when(pl.program_id(1) == 0)
    def _():
        st_ref[...] = jnp.zeros_like(st_ref)

    a_row = -jnp.exp(alog_ref[...])
    ri = lax.broadcasted_iota(jnp.int32, (t, t), 0)
    ci = lax.broadcasted_iota(jnp.int32, (t, t), 1)
    lower = ri >= ci
    tril = lower.astype(F32)

    def chunk(i, carry):
        r0 = pl.multiple_of(i * t, t)
        rows = pl.ds(r0, t)
        dt = dt_ref[rows, :]
        cum, total = _chunk_decays(dt, a_row, tril, n_heads)
        cum_t = (cum * LOG2E).T
        dt_t = dt.T
        ws_t = (dt * jnp.exp(total - cum)).T
        for g in range(n_groups):
            cols = slice(g * gw, (g + 1) * gw)
            xg = xbc_ref[rows, cols]
            bg = xbc_ref[rows, d_inner + g * SSD_STATE:d_inner + (g + 1) * SSD_STATE]
            cg = xbc_ref[rows, d_inner + (n_groups + g) * SSD_STATE:d_inner + (n_groups + g + 1) * SSD_STATE]
            bt = bg.astype(F32).T
            cb = _dot(cg, bt.astype(BF16)).astype(BF16)
            ms, bws, ef, eb = [], [], [], []
            for e in range(gh):
                hf = g * gh + e
                hb = n_heads + hf
                cf_col = _bcol(cum_t, hf)
                cb_col = _bcol(cum_t, hb)
                arg = jnp.where(lower, cf_col - cum_t[hf:hf + 1, :], cb_col - cum_t[hb:hb + 1, :])
                dtf = dt_t[hf:hf + 1, :]
                dtb = dt_t[hb:hb + 1, :]
                w = jnp.where(ri > ci, dtf, jnp.where(ri < ci, dtb, dtf + dtb))
                ms.append((jnp.exp2(arg) * w).astype(BF16) * cb)
                bws.append((bt * ws_t[hf:hf + 1, :]).astype(BF16))
                ef.append(cf_col)
                eb.append(cb_col)
            lhs = jnp.concatenate([jnp.concatenate(ms, axis=1), jnp.concatenate(bws, axis=1)], axis=0)
            r = _dot(lhs, _block_diag(xg))
            e_f = jnp.exp2(_head_lanes(ef))
            st = st_ref[g]
            y_ref[rows, cols] = (r[:t] + xg.astype(F32) * dexp_ref[:, cols] + _dot(cg, st.astype(BF16)) * e_f
                                 + yb_ref[rows, cols] * jnp.exp2(_head_lanes(eb)))
            st_ref[g] = st * e_f[t - 1:t, :] + r[t:]
        return carry

    lax.fori_loop(0, n_chunks, chunk, 0)

    half = y_ref.shape[0] // 2

    def gate_norm(r):
        rows = slice(r * half, (r + 1) * half)
        y = y_ref[rows, :] * _silu(z_ref[rows, :].astype(F32))
        yn_ref[r] = _rmsnorm(y, nw_ref[...]).astype(BF16)

    def project_out(r):
        rows = slice(r * half, (r + 1) * half)
        o_ref[rows, :] = x_ref[rows, :] + _dot(yn_ref[r], wout_ref[...])

    gate_norm(0)
    gate_norm(1)
    project_out(0)
    project_out(1)


def _ssd_scan(x, xbc, dt, z, alog, dexp, nw, wout, *, batch, seq, n_heads, n_groups):
    n, d = x.shape
    d_inner = n_heads * SSD_HEAD_DIM
    tb = min(ROW_TILE, seq)
    bps = seq // tb
    state = pltpu.VMEM((n_groups, SSD_STATE, SSD_GROUP_HEADS * SSD_HEAD_DIM), F32)
    rev = lambda w: pl.BlockSpec((tb, w), lambda b, j: (b * bps + bps - 1 - j, 0))
    fwd = lambda w: pl.BlockSpec((tb, w), lambda b, j: (b * bps + j, 0))

    yb = pl.pallas_call(
        functools.partial(_ssd_bwd_kernel, n_heads=n_heads, n_groups=n_groups),
        grid=(batch, bps),
        in_specs=[rev(xbc.shape[1]), rev(LANES), _resident(alog.shape)],
        out_specs=rev(d_inner),
        out_shape=jax.ShapeDtypeStruct((n, d_inner), F32),
        scratch_shapes=[state],
        compiler_params=_params("arbitrary", "arbitrary"),
        name="ssd_bwd",
    )(xbc, dt, alog)

    return pl.pallas_call(
        functools.partial(_ssd_main_kernel, n_heads=n_heads, n_groups=n_groups),
        grid=(batch, bps),
        in_specs=[fwd(xbc.shape[1]), fwd(LANES), fwd(d_inner), fwd(d_inner), fwd(d),
                  _resident(alog.shape), _resident(dexp.shape), _resident(nw.shape), _resident(wout.shape)],
        out_specs=fwd(d),
        out_shape=jax.ShapeDtypeStruct((n, d), F32),
        scratch_shapes=[state, pltpu.VMEM((tb, d_inner), F32), pltpu.VMEM((2, tb // 2, d_inner), BF16)],
        compiler_params=_params("arbitrary", "arbitrary"),
        name="ssd_main",
    )(xbc, dt, z, yb, x, alog, dexp, nw, wout)


def _attn_qkv_kernel(x_ref, nw_ref, wt_ref, qnt_ref, knt_ref, cost_ref, sint_ref, qt_ref, k_ref, vt_ref, y_ref):
    dh = ATTN_HEAD_DIM
    dq = qt_ref.shape[0]
    dk = vt_ref.shape[0]
    xn = _rmsnorm(x_ref[...], nw_ref[...]).astype(BF16)
    cost = cost_ref[...]
    sint = sint_ref[...]

    def norm_rope(a, w_col, scale):
        an = a * lax.rsqrt(jnp.sum(a * a, axis=0, keepdims=True) * (1.0 / dh) + EPS) * w_col
        x1, x2 = an[:dh // 2], an[dh // 2:]
        return jnp.concatenate([x1 * cost - x2 * sint, x2 * cost + x1 * sint], axis=0) * scale

    q_scale = dh ** -0.5 * LOG2E
    qnt = qnt_ref[...]
    knt = knt_ref[...]
    n_q = dq // MXU_WIDTH

    def project(c):
        rows = min(MXU_WIDTH, dq + dk - c * MXU_WIDTH)
        y_ref[c % 2, 0:rows, :] = _dot_nt(wt_ref[c * MXU_WIDTH:c * MXU_WIDTH + rows, :], xn)

    def finish_q(c):
        for hh in range(MXU_WIDTH // dh):
            qt_ref[c * MXU_WIDTH + hh * dh:c * MXU_WIDTH + (hh + 1) * dh, :] = norm_rope(
                y_ref[c % 2, hh * dh:(hh + 1) * dh, :], qnt, q_scale).astype(BF16)

    assert dq % MXU_WIDTH == 0 and dk <= MXU_WIDTH
    project(0)
    for c in range(n_q):
        project(c + 1)
        finish_q(c)
    vt_ref[...] = _dot_nt(wt_ref[dq + dk:, :], xn).astype(BF16)
    kt = jnp.concatenate(
        [norm_rope(y_ref[n_q % 2, g * dh:(g + 1) * dh, :], knt, 1.0) for g in range(dk // dh)], axis=0)
    k_ref[...] = kt.T.astype(BF16)


def _attn_core_kernel(sink_ref, qt_ref, kp_ref, k_ref, kq_ref, vtp_ref, vt_ref, vtq_ref, x_ref, woutt_ref,
                      o_ref, kbuf_ref, vbuf_ref, obuf_ref, s_ref, p_ref, inv_ref, *, blocks_per_seq, n_kv, group):
    blk = ATTN_BLOCK
    dh = ATTN_HEAD_DIM
    tq = qt_ref.shape[1]
    n_sub = tq // blk
    j = lax.rem(pl.program_id(0), blocks_per_seq)
    kbuf_ref[0:blk, :] = kp_ref[...]
    kbuf_ref[blk:blk + tq, :] = k_ref[...]
    kbuf_ref[blk + tq:, :] = kq_ref[...]
    vbuf_ref[:, 0:blk] = vtp_ref[...]
    vbuf_ref[:, blk:blk + tq] = vt_ref[...]
    vbuf_ref[:, blk + tq:] = vtq_ref[...]

    si = lax.broadcasted_iota(jnp.int32, (3 * blk, blk), 0)
    qi = lax.broadcasted_iota(jnp.int32, (3 * blk, blk), 1)
    rel = si - blk - qi
    band = (rel <= blk) & (rel >= -blk)

    def bias_of(i):
        valid = band
        if i == 0:
            valid = valid & jnp.logical_not((j == 0) & (si < blk))
        if i == n_sub - 1:
            valid = valid & jnp.logical_not((j == blocks_per_seq - 1) & (si >= 2 * blk))
        return jnp.tile(jnp.where(valid, 0.0, -jnp.inf), (1, group))

    def scores(i, g, slot):
        r0 = i * blk
        kk = kbuf_ref[r0:r0 + 3 * blk, (g // 2) * LANES:(g // 2 + 1) * LANES]
        qcat = jnp.concatenate(
            [qt_ref[(g * group + e) * dh:(g * group + e + 1) * dh, r0:r0 + blk] for e in range(group)], axis=1)
        zero = jnp.zeros_like(qcat)
        s_ref[slot] = _dot(kk, jnp.concatenate([qcat, zero] if g % 2 == 0 else [zero, qcat], axis=0))

    def softmax(bias, g, slot):
        s = s_ref[slot] + bias
        sink = jnp.concatenate(
            [jnp.full((1, blk), sink_ref[g * group + e] * LOG2E, F32) for e in range(group)], axis=1)
        m = jnp.maximum(jnp.max(s, axis=0, keepdims=True), sink)
        pr = jnp.exp2(s - m)
        inv_ref[slot] = 1.0 / (jnp.sum(pr, axis=0, keepdims=True) + jnp.exp2(sink - m))
        p_ref[slot] = pr.astype(BF16)

    def values(i, g, slot):
        r0 = i * blk
        ot = _dot(vbuf_ref[g * dh:(g + 1) * dh, r0:r0 + 3 * blk], p_ref[slot]) * inv_ref[slot]
        for e in range(group):
            obuf_ref[(g * group + e) * dh:(g * group + e + 1) * dh, r0:r0 + blk] = (
                ot[:, e * blk:(e + 1) * blk].astype(BF16))

    def project_out(half):
        rows = slice(half * tq // 2, (half + 1) * tq // 2)
        o_ref[rows, :] = x_ref[rows, :] + _dot(woutt_ref[...], obuf_ref[:, rows]).T

    items = [(i, g) for i in range(n_sub) for g in range(n_kv)]
    biases = [bias_of(i) for i in range(n_sub)]
    scores(*items[0], 0)
    for n, (i, g) in enumerate(items):
        if n + 1 < len(items):
            scores(*items[n + 1], (n + 1) % 2)
        softmax(biases[i], g, n % 2)
        if n > 0:
            values(*items[n - 1], (n - 1) % 2)
        if n == len(items) // 2:
            project_out(0)
    values(*items[-1], (len(items) - 1) % 2)
    project_out(1)


def _attention(x, nw, wt, qnt, knt, tables, sink, woutt, *, seq, n_kv):
    n, d = x.shape
    tm = min(ROW_TILE, seq)
    bps = seq // tm
    n_heads = sink.shape[0]
    dq = n_heads * ATTN_HEAD_DIM
    dk = n_kv * ATTN_HEAD_DIM
    assert n_kv % 2 == 0 and wt.shape[0] == dq + 2 * dk
    cost, sint = tables
    row = lambda w: pl.BlockSpec((tm, w), lambda i: (i, 0))
    col = lambda h: pl.BlockSpec((h, tm), lambda i: (0, i))
    post = pl.BlockSpec((ATTN_HEAD_DIM // 2, tm), lambda i: (0, lax.rem(i, bps)))
    qt, k, vt = pl.pallas_call(
        _attn_qkv_kernel,
        grid=(n // tm,),
        in_specs=[row(d), _resident((1, d)), _resident(wt.shape), _resident(qnt.shape), _resident(knt.shape),
                  post, post],
        out_specs=[col(dq), row(dk), col(dk)],
        out_shape=[jax.ShapeDtypeStruct((dq, n), BF16), jax.ShapeDtypeStruct((n, dk), BF16),
                   jax.ShapeDtypeStruct((dk, n), BF16)],
        scratch_shapes=[pltpu.VMEM((2, MXU_WIDTH, tm), F32)],
        compiler_params=_params("arbitrary"),
        name="attn_qkv",
    )(x, nw, wt, qnt, knt, cost, sint)

    sb = tm // ATTN_BLOCK
    last = n // ATTN_BLOCK - 1
    before = lambda i: jnp.maximum(i * sb - 1, 0)
    after = lambda i: jnp.minimum((i + 1) * sb, last)
    return pl.pallas_call(
        functools.partial(_attn_core_kernel, blocks_per_seq=bps, n_kv=n_kv, group=n_heads // n_kv),
        grid=(n // tm,),
        in_specs=[pl.BlockSpec(memory_space=pltpu.SMEM), col(dq),
                  pl.BlockSpec((ATTN_BLOCK, dk), lambda i: (before(i), 0)), row(dk),
                  pl.BlockSpec((ATTN_BLOCK, dk), lambda i: (after(i), 0)),
                  pl.BlockSpec((dk, ATTN_BLOCK), lambda i: (0, before(i))), col(dk),
                  pl.BlockSpec((dk, ATTN_BLOCK), lambda i: (0, after(i))),
                  row(d), _resident(woutt.shape)],
        out_specs=row(d),
        out_shape=jax.ShapeDtypeStruct((n, d), F32),
        scratch_shapes=[pltpu.VMEM((tm + 2 * ATTN_BLOCK, dk), BF16), pltpu.VMEM((dk, tm + 2 * ATTN_BLOCK), BF16),
                        pltpu.VMEM((dq, tm), BF16),
                        pltpu.VMEM((2, 3 * ATTN_BLOCK, n_heads // n_kv * ATTN_BLOCK), F32),
                        pltpu.VMEM((2, 3 * ATTN_BLOCK, n_heads // n_kv * ATTN_BLOCK), BF16),
                        pltpu.VMEM((2, 1, n_heads // n_kv * ATTN_BLOCK), F32)],
        compiler_params=_params("arbitrary"),
        name="attn_core",
    )(sink, qt, k, k, k, vt, vt, vt, x, woutt)


def _rope_tables(seq):
    half = ATTN_HEAD_DIM // 2
    inv_freq = ROPE_THETA ** (-jnp.arange(half, dtype=F32) / half)
    ang = inv_freq[:, None] * jnp.arange(seq, dtype=jnp.int32).astype(F32)[None, :]
    return jnp.cos(ang), jnp.sin(ang)


def kernel(x, norm_w, ffn_w_gate, ffn_w_up, ffn_w_down, ssd_w_in, ssd_conv_w, ssd_conv_b, ssd_dt_bias,
           ssd_a_log, ssd_d, ssd_norm_w, ssd_w_out, attn_w_qkv, attn_q_norm, attn_k_norm, attn_sink,
           attn_w_out):
    batch, seq, d = x.shape
    depth = norm_w.shape[0]
    n_ssd_heads = ssd_d.shape[1]
    d_inner = n_ssd_heads * SSD_HEAD_DIM
    conv_ch = ssd_conv_w.shape[2]
    n_groups = (conv_ch - d_inner) // (2 * SSD_STATE)
    n_kv = (attn_w_qkv.shape[2] // ATTN_HEAD_DIM - attn_sink.shape[1]) // 2
    bf = lambda w: w.astype(BF16)
    pad_lanes = lambda v: jnp.pad(v.astype(F32).reshape(1, -1), ((0, 0), (0, LANES - v.size)))
    tables = _rope_tables(seq)

    h = x.reshape(batch * seq, d)
    for i in range(depth):
        h = _ffn(h, norm_w[i, 0][None], bf(ffn_w_gate[i, 0]), bf(ffn_w_up[i, 0]), bf(ffn_w_down[i, 0]))
        nw = norm_w[i, 1][None]
        j = i // 2
        if i % 2 == 0:
            w_in = ssd_w_in[j]
            wdt = jnp.pad(w_in[:, d_inner + conv_ch:], ((0, 0), (0, LANES - 2 * n_ssd_heads)))
            z, xbc, dt = _ssd_in(h, nw, bf(w_in[:, :d_inner]), bf(w_in[:, d_inner:d_inner + conv_ch]), bf(wdt),
                                 ssd_conv_w[j], ssd_conv_b[j][None], pad_lanes(ssd_dt_bias[j]), seq=seq)
            h = _ssd_scan(h, xbc, dt, z, pad_lanes(ssd_a_log[j]),
                          jnp.repeat(ssd_d[j], SSD_HEAD_DIM)[None], ssd_norm_w[j][None], bf(ssd_w_out[j]),
                          batch=batch, seq=seq, n_heads=n_ssd_heads, n_groups=n_groups)
        else:
            h = _attention(h, nw, bf(attn_w_qkv[j].T), attn_q_norm[j][:, None], attn_k_norm[j][:, None], tables,
                           attn_sink[j], bf(attn_w_out[j].T), seq=seq, n_kv=n_kv)
        h = _ffn(h, norm_w[i, 2][None], bf(ffn_w_gate[i, 1]), bf(ffn_w_up[i, 1]), bf(ffn_w_down[i, 1]))
    return h.reshape(batch, seq, d)
```

```python
import functools

import jax
import jax.numpy as jnp
from jax import lax
from jax.experimental import pallas as pl
from jax.experimental.pallas import tpu as pltpu

F32 = jnp.float32
BF16 = jnp.bfloat16
EPS = 1e-6
ROPE_THETA = 10000.0

V7X_VMEM_LIMIT_BYTES = 56 * 1024 * 1024
LANES = 128
HALO = 8

SSD_CHUNK = 128
SSD_HEAD_DIM = 64
SSD_STATE = 128
SSD_GROUP_HEADS = 4
SSD_CONV = 5
ATTN_HEAD_DIM = 64
ATTN_BLOCK = 128

ROW_TILE = 512
FFN_ROW_TILE = 1024
MXU_WIDTH = 256
CONV_PHASES = 4
LOG2E = 1.4426950408889634


def _params(*semantics):
    return pltpu.CompilerParams(dimension_semantics=semantics, vmem_limit_bytes=V7X_VMEM_LIMIT_BYTES)


def _resident(shape):
    return pl.BlockSpec(shape, lambda *_: (0,) * len(shape), pipeline_mode=pl.Buffered(1))


def _rmsnorm(x, w):
    return x * lax.rsqrt(jnp.mean(x * x, axis=-1, keepdims=True) + EPS) * w


def _silu(x):
    h = 0.5 * x
    return h + h * jnp.tanh(h)


def _dot(a, b):
    return jnp.dot(a, b, preferred_element_type=F32)


def _dot_nt(a, b):
    return lax.dot_general(a, b, (((1,), (1,)), ((), ())), preferred_element_type=F32)


def _ffn_kernel(x_ref, nw_ref, wg_ref, wu_ref, wd_ref, o_ref, h_ref, *, ff_chunk):
    x = x_ref[...]
    xn = _rmsnorm(x, nw_ref[...]).astype(BF16)
    for c in range(wg_ref.shape[1] // ff_chunk):
        sl = slice(c * ff_chunk, (c + 1) * ff_chunk)
        g = _dot(xn, wg_ref[:, sl])
        u = _dot(xn, wu_ref[:, sl])
        h_ref[:, sl] = (_silu(g) * u).astype(BF16)
    o_ref[...] = x + 0.5 * _dot(h_ref[...], wd_ref[...])


def _ffn(x, nw, wg, wu, wd, layer, half):
    n, d = x.shape
    d_ff = wg.shape[-1]
    tm = min(FFN_ROW_TILE, n)
    row = pl.BlockSpec((tm, d), lambda i: (i, 0))
    pick = lambda r, c: pl.BlockSpec((None, None, r, c), lambda i: (layer, half, 0, 0),
                                     pipeline_mode=pl.Buffered(1))
    return pl.pallas_call(
        functools.partial(_ffn_kernel, ff_chunk=MXU_WIDTH),
        grid=(n // tm,),
        in_specs=[row, _resident((1, d)), pick(d, d_ff), pick(d, d_ff), pick(d_ff, d)],
        out_specs=row,
        out_shape=jax.ShapeDtypeStruct((n, d), F32),
        scratch_shapes=[pltpu.VMEM((tm, d_ff), BF16)],
        compiler_params=_params("arbitrary"),
        name="ffn",
    )(x, nw, wg, wu, wd)


def _ssd_in_kernel(xp_ref, x_ref, xq_ref, nw_ref, w_ref, cw_ref, cb_ref, dtb_ref,
                   z_ref, xbc_ref, dt_ref, p_ref, a_ref, *, blocks_per_seq, col_chunk):
    tm = x_ref.shape[0]
    d_z = z_ref.shape[1]
    d_xbc = xbc_ref.shape[1]
    j = lax.rem(pl.program_id(0), blocks_per_seq)
    nw = nw_ref[...]
    xm = _rmsnorm(x_ref[...], nw)
    xp = _rmsnorm(xp_ref[...], nw) * (j != 0).astype(F32)
    xq = _rmsnorm(xq_ref[...], nw) * (j != blocks_per_seq - 1).astype(F32)
    xm_bf = xm.astype(BF16)
    xe_bf = jnp.concatenate([xp, xm, xq], axis=0).astype(BF16)

    slabs = col_chunk // LANES
    n_chunks = d_xbc // col_chunk

    def project(c):
        u = _dot(xe_bf, w_ref[:, d_z + c * col_chunk:d_z + (c + 1) * col_chunk])
        for s in range(slabs):
            p_ref[c % 2, s] = u[:, s * LANES:(s + 1) * LANES]

    def conv(c):
        for s in range(slabs):
            lanes = slice(c * col_chunk + s * LANES, c * col_chunk + (s + 1) * LANES)
            for ph in range(CONV_PHASES):
                a = cb_ref[:, lanes]
                for k in range(SSD_CONV):
                    a = a + cw_ref[k:k + 1, lanes] * p_ref[
                        c % 2, s, pl.ds(ph + HALO - SSD_CONV // 2 + k, tm // CONV_PHASES, stride=CONV_PHASES), :]
                a_ref[s, pl.ds(ph, tm // CONV_PHASES, stride=CONV_PHASES), :] = _silu(a)
            xbc_ref[:, lanes] = a_ref[s].astype(BF16)

    def project_z(c):
        sl = slice(c * col_chunk, (c + 1) * col_chunk)
        z_ref[:, sl] = _dot(xm_bf, w_ref[:, sl]).astype(BF16)

    z_chunks = list(range(d_z // col_chunk))
    project(0)
    for c in range(n_chunks):
        if c + 1 < n_chunks:
            project(c + 1)
        if c % 2 == 1 and z_chunks:
            project_z(z_chunks.pop(0))
        conv(c)
    dt_raw = _dot(xm_bf, w_ref[:, d_z + d_xbc:])
    dt_raw = jnp.concatenate([dt_raw, jnp.zeros((tm, LANES - dt_raw.shape[1]), F32)], axis=1)
    dt_ref[...] = jax.nn.softplus(dt_raw + dtb_ref[...])
    for c in z_chunks:
        project_z(c)


def _ssd_in(x, nw, w_in, cw, cb, dtb, *, seq, d_z):
    n, d = x.shape
    d_xbc = cw.shape[1]
    tm = min(ROW_TILE, seq)
    bps = seq // tm
    hb = tm // HALO
    last_halo = n // HALO - 1
    row = lambda w: pl.BlockSpec((tm, w), lambda i: (i, 0))
    prev = pl.BlockSpec((HALO, d), lambda i: (jnp.maximum(i * hb - 1, 0), 0))
    nxt = pl.BlockSpec((HALO, d), lambda i: (jnp.minimum((i + 1) * hb, last_halo), 0))
    col_chunk = 2 * MXU_WIDTH
    return pl.pallas_call(
        functools.partial(_ssd_in_kernel, blocks_per_seq=bps, col_chunk=col_chunk),
        grid=(n // tm,),
        in_specs=[prev, row(d), nxt, _resident((1, d)), _resident(w_in.shape),
                  _resident(cw.shape), _resident(cb.shape), _resident(dtb.shape)],
        out_specs=[row(d_z), row(d_xbc), row(LANES)],
        out_shape=[jax.ShapeDtypeStruct((n, d_z), BF16),
                   jax.ShapeDtypeStruct((n, d_xbc), BF16),
                   jax.ShapeDtypeStruct((n, LANES), F32)],
        scratch_shapes=[pltpu.VMEM((2, col_chunk // LANES, tm + 2 * HALO, LANES), F32),
                        pltpu.VMEM((col_chunk // LANES, tm, LANES), F32)],
        compiler_params=_params("arbitrary"),
        name="ssd_in",
    )(x, x, x, nw, w_in, cw, cb, dtb)


def _chunk_decays(dt, a_row, tril, n_heads):
    t = dt.shape[0]
    da = dt * a_row
    prefix = jnp.dot(tril, da, precision=lax.Precision.HIGHEST, preferred_element_type=F32)
    total = prefix[t - 1:t, :]
    lane = lax.broadcasted_iota(jnp.int32, dt.shape, 1)
    return jnp.where(lane < n_heads, prefix, total - prefix + da), total


def _bcol(v_t, j):
    return jnp.broadcast_to(v_t[j:j + 1, :], (LANES, v_t.shape[1])).T


def _head_lanes(cols):
    first = lax.broadcasted_iota(jnp.int32, cols[0].shape, 1) < SSD_HEAD_DIM
    return jnp.concatenate([jnp.where(first, cols[0], cols[1]), jnp.where(first, cols[2], cols[3])], axis=1)


def _block_diag(xg):
    lane = lax.broadcasted_iota(jnp.int32, xg.shape, 1)
    zero = jnp.zeros_like(xg)
    return jnp.concatenate(
        [jnp.where((lane >= e * SSD_HEAD_DIM) & (lane < (e + 1) * SSD_HEAD_DIM), xg, zero)
         for e in range(SSD_GROUP_HEADS)], axis=0)


def _ssd_bwd_kernel(xbc_ref, dt_ref, alog_ref, yb_ref, st_ref, *, n_heads, n_groups):
    t = SSD_CHUNK
    gh = SSD_GROUP_HEADS
    gw = gh * SSD_HEAD_DIM
    d_inner = n_heads * SSD_HEAD_DIM
    n_chunks = xbc_ref.shape[0] // t

    @pl.when(pl.program_id(1) == 0)
    def _():
        st_ref[...] = jnp.zeros_like(st_ref)

    a_row = -jnp.exp(alog_ref[...])
    ri = lax.broadcasted_iota(jnp.int32, (t, t), 0)
    ci = lax.broadcasted_iota(jnp.int32, (t, t), 1)
    tril = (ri >= ci).astype(F32)

    def chunk(i, carry):
        r0 = pl.multiple_of((n_chunks - 1 - i) * t, t)
        rows = pl.ds(r0, t)
        dt = dt_ref[rows, :]
        cum, total = _chunk_decays(dt, a_row, tril, n_heads)
        ws_t = (dt * jnp.exp(total - cum)).T
        decay = jnp.exp(total)
        for g in range(n_groups):
            h0 = n_heads + g * gh
            xg = xbc_ref[rows, g * gw:(g + 1) * gw]
            bg = xbc_ref[rows, d_inner + g * SSD_STATE:d_inner + (g + 1) * SSD_STATE]
            cg = xbc_ref[rows, d_inner + (n_groups + g) * SSD_STATE:d_inner + (n_groups + g + 1) * SSD_STATE]
            st = st_ref[g]
            yb_ref[rows, g * gw:(g + 1) * gw] = _dot(cg, st.astype(BF16))
            bt = bg.astype(F32).T
            bws = [(bt * ws_t[h0 + e:h0 + e + 1, :]).astype(BF16) for e in range(gh)]
            decay_g = _head_lanes([jnp.broadcast_to(decay[:, h0 + e:h0 + e + 1], (1, LANES)) for e in range(gh)])
            st_ref[g] = st * decay_g + _dot(jnp.concatenate(bws, axis=1), _block_diag(xg))
        return carry

    lax.fori_loop(0, n_chunks, chunk, 0)


def _ssd_main_kernel(xbc_ref, dt_ref, z_ref, yb_ref, x_ref, alog_ref, dexp_ref, nw_ref, wout_ref,
                     o_ref, st_ref, y_ref, yn_ref, *, n_heads, n_groups):
    t = SSD_CHUNK
    gh = SSD_GROUP_HEADS
    gw = gh * SSD_HEAD_DIM
    d_inner = n_heads * SSD_HEAD_DIM
    n_chunks = xbc_ref.shape[0] // t

    @pl.when(pl.program_id(1) == 0)
    def _():
        st_ref[...] = jnp.zeros_like(st_ref)

    a_row = -jnp.exp(alog_ref[...])
    ri = lax.broadcasted_iota(jnp.int32, (t, t), 0)
    ci = lax.broadcasted_iota(jnp.int32, (t, t), 1)
    lower = ri >= ci
    tril = lower.astype(F32)

    def chunk(i, carry):
        r0 = pl.multiple_of(i * t, t)
        rows = pl.ds(r0, t)
        dt = dt_ref[rows, :]
        cum, total = _chunk_decays(dt, a_row, tril, n_heads)
        cum_t = (cum * LOG2E).T
        dt_t = dt.T
        ws_t = (dt * jnp.exp(total - cum)).T
        for g in range(n_groups):
            cols = slice(g * gw, (g + 1) * gw)
            xg = xbc_ref[rows, cols]
            bg = xbc_ref[rows, d_inner + g * SSD_STATE:d_inner + (g + 1) * SSD_STATE]
            cg = xbc_ref[rows, d_inner + (n_groups + g) * SSD_STATE:d_inner + (n_groups + g + 1) * SSD_STATE]
            bt = bg.astype(F32).T
            cb = _dot(cg, bt.astype(BF16)).astype(BF16)
            ms, bws, ef, eb = [], [], [], []
            for e in range(gh):
                hf = g * gh + e
                hb = n_heads + hf
                cf_col = _bcol(cum_t, hf)
                cb_col = _bcol(cum_t, hb)
                arg = jnp.where(lower, cf_col - cum_t[hf:hf + 1, :], cb_col - cum_t[hb:hb + 1, :])
                dtf = dt_t[hf:hf + 1, :]
                dtb = dt_t[hb:hb + 1, :]
                w = jnp.where(ri > ci, dtf, jnp.where(ri < ci, dtb, dtf + dtb))
                ms.append((jnp.exp2(arg) * w).astype(BF16) * cb)
                bws.append((bt * ws_t[hf:hf + 1, :]).astype(BF16))
                ef.append(cf_col)
                eb.append(cb_col)
            lhs = jnp.concatenate([jnp.concatenate(ms, axis=1), jnp.concatenate(bws, axis=1)], axis=0)
            r = _dot(lhs, _block_diag(xg))
            e_f = jnp.exp2(_head_lanes(ef))
            st = st_ref[g]
            y_ref[rows, cols] = (r[:t] + xg.astype(F32) * dexp_ref[:, cols] + _dot(cg, st.astype(BF16)) * e_f
                                 + yb_ref[rows, cols] * jnp.exp2(_head_lanes(eb)))
            st_ref[g] = st * e_f[t - 1:t, :] + r[t:]
        return carry

    lax.fori_loop(0, n_chunks, chunk, 0)

    half = y_ref.shape[0] // 2

    def gate_norm(r):
        rows = slice(r * half, (r + 1) * half)
        y = y_ref[rows, :] * _silu(z_ref[rows, :].astype(F32))
        yn_ref[r] = _rmsnorm(y, nw_ref[...]).astype(BF16)

    def project_out(r):
        rows = slice(r * half, (r + 1) * half)
        o_ref[rows, :] = x_ref[rows, :] + _dot(yn_ref[r], wout_ref[...])

    gate_norm(0)
    gate_norm(1)
    project_out(0)
    project_out(1)


def _ssd_scan(x, xbc, dt, z, alog, dexp, nw, wout, *, batch, seq, n_heads, n_groups):
    n, d = x.shape
    d_inner = n_heads * SSD_HEAD_DIM
    tb = min(ROW_TILE, seq)
    bps = seq // tb
    state = pltpu.VMEM((n_groups, SSD_STATE, SSD_GROUP_HEADS * SSD_HEAD_DIM), F32)
    rev = lambda w: pl.BlockSpec((tb, w), lambda b, j: (b * bps + bps - 1 - j, 0))

    yb = pl.pallas_call(
        functools.partial(_ssd_bwd_kernel, n_heads=n_heads, n_groups=n_groups),
        grid=(batch, bps),
        in_specs=[rev(xbc.shape[1]), rev(LANES), _resident(alog.shape)],
        out_specs=rev(d_inner),
        out_shape=jax.ShapeDtypeStruct((n, d_inner), F32),
        scratch_shapes=[state],
        compiler_params=_params("arbitrary", "arbitrary"),
        name="ssd_bwd",
    )(xbc, dt, alog)

    fwd = lambda w: pl.BlockSpec((tb, w), lambda b, j: (b * bps + j, 0))
    return pl.pallas_call(
        functools.partial(_ssd_main_kernel, n_heads=n_heads, n_groups=n_groups),
        grid=(batch, bps),
        in_specs=[fwd(xbc.shape[1]), fwd(LANES), fwd(d_inner), fwd(d_inner), fwd(d),
                  _resident(alog.shape), _resident(dexp.shape), _resident(nw.shape), _resident(wout.shape)],
        out_specs=fwd(d),
        out_shape=jax.ShapeDtypeStruct((n, d), F32),
        scratch_shapes=[state, pltpu.VMEM((tb, d_inner), F32), pltpu.VMEM((2, tb // 2, d_inner), BF16)],
        compiler_params=_params("arbitrary", "arbitrary"),
        name="ssd_main",
    )(xbc, dt, z, yb, x, alog, dexp, nw, wout)


def _attn_qkv_kernel(x_ref, nw_ref, wt_ref, qnt_ref, knt_ref, cost_ref, sint_ref, qt_ref, k_ref, vt_ref, y_ref):
    dh = ATTN_HEAD_DIM
    dq = qt_ref.shape[0]
    dk = vt_ref.shape[0]
    xn = _rmsnorm(x_ref[...], nw_ref[...]).astype(BF16)
    cost = cost_ref[...]
    sint = sint_ref[...]

    def norm_rope(a, w_col, scale):
        an = a * lax.rsqrt(jnp.sum(a * a, axis=0, keepdims=True) * (1.0 / dh) + EPS) * w_col
        x1, x2 = an[:dh // 2], an[dh // 2:]
        return jnp.concatenate([x1 * cost - x2 * sint, x2 * cost + x1 * sint], axis=0) * scale

    q_scale = dh ** -0.5 * LOG2E
    qnt = qnt_ref[...]
    knt = knt_ref[...]
    n_q = dq // MXU_WIDTH

    def project(c):
        rows = min(MXU_WIDTH, dq + dk - c * MXU_WIDTH)
        y_ref[c % 2, 0:rows, :] = _dot_nt(wt_ref[c * MXU_WIDTH:c * MXU_WIDTH + rows, :], xn)

    def finish_q(c):
        for hh in range(MXU_WIDTH // dh):
            qt_ref[c * MXU_WIDTH + hh * dh:c * MXU_WIDTH + (hh + 1) * dh, :] = norm_rope(
                y_ref[c % 2, hh * dh:(hh + 1) * dh, :], qnt, q_scale).astype(BF16)

    assert dq % MXU_WIDTH == 0 and dk <= MXU_WIDTH
    project(0)
    for c in range(n_q):
        project(c + 1)
        finish_q(c)
    vt_ref[...] = _dot_nt(wt_ref[dq + dk:, :], xn).astype(BF16)
    kt = jnp.concatenate(
        [norm_rope(y_ref[n_q % 2, g * dh:(g + 1) * dh, :], knt, 1.0) for g in range(dk // dh)], axis=0)
    k_ref[...] = kt.T.astype(BF16)


def _attn_core_kernel(sink_ref, qt_ref, kp_ref, k_ref, kq_ref, vtp_ref, vt_ref, vtq_ref, x_ref, woutt_ref,
                      o_ref, kbuf_ref, vbuf_ref, obuf_ref, s_ref, p_ref, inv_ref, *, blocks_per_seq, n_kv, group):
    blk = ATTN_BLOCK
    dh = ATTN_HEAD_DIM
    tq = qt_ref.shape[1]
    n_sub = tq // blk
    j = lax.rem(pl.program_id(0), blocks_per_seq)
    kbuf_ref[0:blk, :] = kp_ref[...]
    kbuf_ref[blk:blk + tq, :] = k_ref[...]
    kbuf_ref[blk + tq:, :] = kq_ref[...]
    vbuf_ref[:, 0:blk] = vtp_ref[...]
    vbuf_ref[:, blk:blk + tq] = vt_ref[...]
    vbuf_ref[:, blk + tq:] = vtq_ref[...]

    si = lax.broadcasted_iota(jnp.int32, (3 * blk, blk), 0)
    qi = lax.broadcasted_iota(jnp.int32, (3 * blk, blk), 1)
    rel = si - blk - qi
    band = (rel <= blk) & (rel >= -blk)

    def bias_of(i):
        valid = band
        if i == 0:
            valid = valid & jnp.logical_not((j == 0) & (si < blk))
        if i == n_sub - 1:
            valid = valid & jnp.logical_not((j == blocks_per_seq - 1) & (si >= 2 * blk))
        return jnp.tile(jnp.where(valid, 0.0, -jnp.inf), (1, group))

    def scores(i, g, slot):
        r0 = i * blk
        kk = kbuf_ref[r0:r0 + 3 * blk, (g // 2) * LANES:(g // 2 + 1) * LANES]
        qcat = jnp.concatenate(
            [qt_ref[(g * group + e) * dh:(g * group + e + 1) * dh, r0:r0 + blk] for e in range(group)], axis=1)
        zero = jnp.zeros_like(qcat)
        s_ref[slot] = _dot(kk, jnp.concatenate([qcat, zero] if g % 2 == 0 else [zero, qcat], axis=0))

    def softmax(bias, g, slot):
        s = s_ref[slot] + bias
        sink = jnp.concatenate(
            [jnp.full((1, blk), sink_ref[g * group + e] * LOG2E, F32) for e in range(group)], axis=1)
        m = jnp.maximum(jnp.max(s, axis=0, keepdims=True), sink)
        pr = jnp.exp2(s - m)
        inv_ref[slot] = 1.0 / (jnp.sum(pr, axis=0, keepdims=True) + jnp.exp2(sink - m))
        p_ref[slot] = pr.astype(BF16)

    def values(i, g, slot):
        r0 = i * blk
        ot = _dot(vbuf_ref[g * dh:(g + 1) * dh, r0:r0 + 3 * blk], p_ref[slot]) * inv_ref[slot]
        for e in range(group):
            obuf_ref[(g * group + e) * dh:(g * group + e + 1) * dh, r0:r0 + blk] = (
                ot[:, e * blk:(e + 1) * blk].astype(BF16))

    def project_out(half):
        rows = slice(half * tq // 2, (half + 1) * tq // 2)
        o_ref[rows, :] = x_ref[rows, :] + _dot(woutt_ref[...], obuf_ref[:, rows]).T

    items = [(i, g) for i in range(n_sub) for g in range(n_kv)]
    biases = [bias_of(i) for i in range(n_sub)]
    scores(*items[0], 0)
    for n, (i, g) in enumerate(items):
        if n + 1 < len(items):
            scores(*items[n + 1], (n + 1) % 2)
        softmax(biases[i], g, n % 2)
        if n > 0:
            values(*items[n - 1], (n - 1) % 2)
        if n == len(items) // 2:
            project_out(0)
    values(*items[-1], (len(items) - 1) % 2)
    project_out(1)


def _attention(x, nw, wt, qnt, knt, tables, sink, woutt, *, seq, n_kv):
    n, d = x.shape
    tm = min(ROW_TILE, seq)
    bps = seq // tm
    n_heads = sink.shape[0]
    dq = n_heads * ATTN_HEAD_DIM
    dk = n_kv * ATTN_HEAD_DIM
    assert n_kv % 2 == 0 and wt.shape[0] == dq + 2 * dk
    cost, sint = tables
    row = lambda w: pl.BlockSpec((tm, w), lambda i: (i, 0))
    col = lambda h: pl.BlockSpec((h, tm), lambda i: (0, i))
    post = pl.BlockSpec((ATTN_HEAD_DIM // 2, tm), lambda i: (0, lax.rem(i, bps)))
    qt, k, vt = pl.pallas_call(
        _attn_qkv_kernel,
        grid=(n // tm,),
        in_specs=[row(d), _resident((1, d)), _resident(wt.shape), _resident(qnt.shape), _resident(knt.shape),
                  post, post],
        out_specs=[col(dq), row(dk), col(dk)],
        out_shape=[jax.ShapeDtypeStruct((dq, n), BF16), jax.ShapeDtypeStruct((n, dk), BF16),
                   jax.ShapeDtypeStruct((dk, n), BF16)],
        scratch_shapes=[pltpu.VMEM((2, MXU_WIDTH, tm), F32)],
        compiler_params=_params("arbitrary"),
        name="attn_qkv",
    )(x, nw, wt, qnt, knt, cost, sint)

    sb = tm // ATTN_BLOCK
    last = n // ATTN_BLOCK - 1
    before = lambda i: jnp.maximum(i * sb - 1, 0)
    after = lambda i: jnp.minimum((i + 1) * sb, last)
    return pl.pallas_call(
        functools.partial(_attn_core_kernel, blocks_per_seq=bps, n_kv=n_kv, group=n_heads // n_kv),
        grid=(n // tm,),
        in_specs=[pl.BlockSpec(memory_space=pltpu.SMEM), col(dq),
                  pl.BlockSpec((ATTN_BLOCK, dk), lambda i: (before(i), 0)), row(dk),
                  pl.BlockSpec((ATTN_BLOCK, dk), lambda i: (after(i), 0)),
                  pl.BlockSpec((dk, ATTN_BLOCK), lambda i: (0, before(i))), col(dk),
                  pl.BlockSpec((dk, ATTN_BLOCK), lambda i: (0, after(i))),
                  row(d), _resident(woutt.shape)],
        out_specs=row(d),
        out_shape=jax.ShapeDtypeStruct((n, d), F32),
        scratch_shapes=[pltpu.VMEM((tm + 2 * ATTN_BLOCK, dk), BF16), pltpu.VMEM((dk, tm + 2 * ATTN_BLOCK), BF16),
                        pltpu.VMEM((dq, tm), BF16),
                        pltpu.VMEM((2, 3 * ATTN_BLOCK, n_heads // n_kv * ATTN_BLOCK), F32),
                        pltpu.VMEM((2, 3 * ATTN_BLOCK, n_heads // n_kv * ATTN_BLOCK), BF16),
                        pltpu.VMEM((2, 1, n_heads // n_kv * ATTN_BLOCK), F32)],
        compiler_params=_params("arbitrary"),
        name="attn_core",
    )(sink, qt, k, k, k, vt, vt, vt, x, woutt)


def _rope_tables(seq):
    half = ATTN_HEAD_DIM // 2
    inv_freq = ROPE_THETA ** (-jnp.arange(half, dtype=F32) / half)
    ang = inv_freq[:, None] * jnp.arange(seq, dtype=jnp.int32).astype(F32)[None, :]
    return jnp.cos(ang), jnp.sin(ang)


def kernel(x, norm_w, ffn_w_gate, ffn_w_up, ffn_w_down, ssd_w_in, ssd_conv_w, ssd_conv_b, ssd_dt_bias,
           ssd_a_log, ssd_d, ssd_norm_w, ssd_w_out, attn_w_qkv, attn_q_norm, attn_k_norm, attn_sink,
           attn_w_out):
    batch, seq, d = x.shape
    depth = norm_w.shape[0]
    n_ssd_heads = ssd_d.shape[1]
    d_inner = n_ssd_heads * SSD_HEAD_DIM
    conv_ch = ssd_conv_w.shape[2]
    n_groups = (conv_ch - d_inner) // (2 * SSD_STATE)
    n_kv = (attn_w_qkv.shape[2] // ATTN_HEAD_DIM - attn_sink.shape[1]) // 2
    bf = lambda w: w.astype(BF16)
    pad_lanes = lambda v: jnp.pad(v.astype(F32).reshape(1, -1), ((0, 0), (0, LANES - v.size)))
    tables = _rope_tables(seq)

    wg, wu, wd = bf(ffn_w_gate), bf(ffn_w_up), bf(ffn_w_down)
    h = x.reshape(batch * seq, d)
    for i in range(depth):
        h = _ffn(h, norm_w[i, 0][None], wg, wu, wd, i, 0)
        nw = norm_w[i, 1][None]
        j = i // 2
        if i % 2 == 0:
            z, xbc, dt = _ssd_in(h, nw, bf(ssd_w_in[j]), ssd_conv_w[j], ssd_conv_b[j][None],
                                 pad_lanes(ssd_dt_bias[j]), seq=seq, d_z=d_inner)
            h = _ssd_scan(h, xbc, dt, z, pad_lanes(ssd_a_log[j]),
                          jnp.repeat(ssd_d[j], SSD_HEAD_DIM)[None], ssd_norm_w[j][None], bf(ssd_w_out[j]),
                          batch=batch, seq=seq, n_heads=n_ssd_heads, n_groups=n_groups)
        else:
            h = _attention(h, nw, bf(attn_w_qkv[j].T), attn_q_norm[j][:, None], attn_k_norm[j][:, None], tables,
                           attn_sink[j], bf(attn_w_out[j].T), seq=seq, n_kv=n_kv)
        h = _ffn(h, norm_w[i, 2][None], wg, wu, wd, i, 1)
    return h.reshape(batch, seq, d)
```

```python
import functools

import jax
import jax.numpy as jnp
from jax import lax
from jax.experimental import pallas as pl
from jax.experimental.pallas import tpu as pltpu

F32 = jnp.float32
BF16 = jnp.bfloat16
EPS = 1e-6
ROPE_THETA = 10000.0

V7X_VMEM_LIMIT_BYTES = 56 * 1024 * 1024
LANES = 128
HALO = 8

SSD_CHUNK = 128
SSD_HEAD_DIM = 64
SSD_STATE = 128
SSD_GROUP_HEADS = 4
SSD_CONV = 5
ATTN_HEAD_DIM = 64
ATTN_BLOCK = 128

ROW_TILE = 512
FFN_ROW_TILE = 1024
MXU_WIDTH = 256
CONV_PHASES = 4
LOG2E = 1.4426950408889634


def _params(*semantics):
    return pltpu.CompilerParams(dimension_semantics=semantics, vmem_limit_bytes=V7X_VMEM_LIMIT_BYTES)


def _resident(shape):
    return pl.BlockSpec(shape, lambda *_: (0,) * len(shape), pipeline_mode=pl.Buffered(1))


def _rmsnorm(x, w):
    return x * lax.rsqrt(jnp.mean(x * x, axis=-1, keepdims=True) + EPS) * w


def _silu(x):
    h = 0.5 * x
    return h + h * jnp.tanh(h)


def _dot(a, b):
    return jnp.dot(a, b, preferred_element_type=F32)


def _dot_nt(a, b):
    return lax.dot_general(a, b, (((1,), (1,)), ((), ())), preferred_element_type=F32)


def _ffn_kernel(x_ref, nw_ref, wg_ref, wu_ref, wd_ref, o_ref, h_ref, *, ff_chunk):
    x = x_ref[...]
    xn = _rmsnorm(x, nw_ref[...]).astype(BF16)
    for c in range(wg_ref.shape[1] // ff_chunk):
        sl = slice(c * ff_chunk, (c + 1) * ff_chunk)
        g = _dot(xn, wg_ref[:, sl])
        u = _dot(xn, wu_ref[:, sl])
        h_ref[:, sl] = (_silu(g) * u).astype(BF16)
    o_ref[...] = x + 0.5 * _dot(h_ref[...], wd_ref[...])


def _ffn(x, nw, wg, wu, wd, layer, half):
    n, d = x.shape
    d_ff = wg.shape[-1]
    tm = min(FFN_ROW_TILE, n)
    row = pl.BlockSpec((tm, d), lambda i: (i, 0))
    pick = lambda r, c: pl.BlockSpec((None, None, r, c), lambda i: (layer, half, 0, 0),
                                     pipeline_mode=pl.Buffered(1))
    return pl.pallas_call(
        functools.partial(_ffn_kernel, ff_chunk=MXU_WIDTH),
        grid=(n // tm,),
        in_specs=[row, _resident((1, d)), pick(d, d_ff), pick(d, d_ff), pick(d_ff, d)],
        out_specs=row,
        out_shape=jax.ShapeDtypeStruct((n, d), F32),
        scratch_shapes=[pltpu.VMEM((tm, d_ff), BF16)],
        compiler_params=_params("arbitrary"),
        name="ffn",
    )(x, nw, wg, wu, wd)


def _ssd_in_kernel(xp_ref, x_ref, xq_ref, nw_ref, w_ref, cw_ref, cb_ref, dtb_ref,
                   z_ref, xbc_ref, dt_ref, p_ref, a_ref, *, blocks_per_seq, col_chunk):
    tm = x_ref.shape[0]
    d_z = z_ref.shape[1]
    d_xbc = xbc_ref.shape[1]
    j = lax.rem(pl.program_id(0), blocks_per_seq)
    nw = nw_ref[...]
    xm = _rmsnorm(x_ref[...], nw)
    xp = _rmsnorm(xp_ref[...], nw) * (j != 0).astype(F32)
    xq = _rmsnorm(xq_ref[...], nw) * (j != blocks_per_seq - 1).astype(F32)
    xm_bf = xm.astype(BF16)
    xe_bf = jnp.concatenate([xp, xm, xq], axis=0).astype(BF16)

    slabs = col_chunk // LANES
    n_chunks = d_xbc // col_chunk

    def project(c):
        u = _dot(xe_bf, w_ref[:, d_z + c * col_chunk:d_z + (c + 1) * col_chunk])
        for s in range(slabs):
            p_ref[c % 2, s] = u[:, s * LANES:(s + 1) * LANES]

    def conv(c):
        for s in range(slabs):
            lanes = slice(c * col_chunk + s * LANES, c * col_chunk + (s + 1) * LANES)
            for ph in range(CONV_PHASES):
                a = cb_ref[:, lanes]
                for k in range(SSD_CONV):
                    a = a + cw_ref[k:k + 1, lanes] * p_ref[
                        c % 2, s, pl.ds(ph + HALO - SSD_CONV // 2 + k, tm // CONV_PHASES, stride=CONV_PHASES), :]
                a_ref[s, pl.ds(ph, tm // CONV_PHASES, stride=CONV_PHASES), :] = _silu(a)
            xbc_ref[:, lanes] = a_ref[s].astype(BF16)

    def project_z(c):
        sl = slice(c * col_chunk, (c + 1) * col_chunk)
        z_ref[:, sl] = _dot(xm_bf, w_ref[:, sl]).astype(BF16)

    z_chunks = list(range(d_z // col_chunk))
    project(0)
    for c in range(n_chunks):
        if c + 1 < n_chunks:
            project(c + 1)
        if c % 2 == 1 and z_chunks:
            project_z(z_chunks.pop(0))
        conv(c)
    dt_raw = _dot(xm_bf, w_ref[:, d_z + d_xbc:])
    dt_raw = jnp.concatenate([dt_raw, jnp.zeros((tm, LANES - dt_raw.shape[1]), F32)], axis=1)
    dt_ref[...] = jax.nn.softplus(dt_raw + dtb_ref[...])
    for c in z_chunks:
        project_z(c)


def _ssd_in(x, nw, w_in, cw, cb, dtb, *, seq, d_z):
    n, d = x.shape
    d_xbc = cw.shape[1]
    tm = min(ROW_TILE, seq)
    bps = seq // tm
    hb = tm // HALO
    last_halo = n // HALO - 1
    row = lambda w: pl.BlockSpec((tm, w), lambda i: (i, 0))
    prev = pl.BlockSpec((HALO, d), lambda i: (jnp.maximum(i * hb - 1, 0), 0))
    nxt = pl.BlockSpec((HALO, d), lambda i: (jnp.minimum((i + 1) * hb, last_halo), 0))
    col_chunk = 2 * MXU_WIDTH
    return pl.pallas_call(
        functools.partial(_ssd_in_kernel, blocks_per_seq=bps, col_chunk=col_chunk),
        grid=(n // tm,),
        in_specs=[prev, row(d), nxt, _resident((1, d)), _resident(w_in.shape),
                  _resident(cw.shape), _resident(cb.shape), _resident(dtb.shape)],
        out_specs=[row(d_z), row(d_xbc), row(LANES)],
        out_shape=[jax.ShapeDtypeStruct((n, d_z), BF16),
                   jax.ShapeDtypeStruct((n, d_xbc), BF16),
                   jax.ShapeDtypeStruct((n, LANES), F32)],
        scratch_shapes=[pltpu.VMEM((2, col_chunk // LANES, tm + 2 * HALO, LANES), F32),
                        pltpu.VMEM((col_chunk // LANES, tm, LANES), F32)],
        compiler_params=_params("arbitrary"),
        name="ssd_in",
    )(x, x, x, nw, w_in, cw, cb, dtb)


def _chunk_decays(dt, a_row, tril, n_heads):
    t = dt.shape[0]
    da = dt * a_row
    prefix = jnp.dot(tril, da, precision=lax.Precision.HIGHEST, preferred_element_type=F32)
    total = prefix[t - 1:t, :]
    lane = lax.broadcasted_iota(jnp.int32, dt.shape, 1)
    return jnp.where(lane < n_heads, prefix, total - prefix + da), total


def _bcol(v_t, j):
    return jnp.broadcast_to(v_t[j:j + 1, :], (LANES, v_t.shape[1])).T


def _head_lanes(cols):
    first = lax.broadcasted_iota(jnp.int32, cols[0].shape, 1) < SSD_HEAD_DIM
    return jnp.concatenate([jnp.where(first, cols[0], cols[1]), jnp.where(first, cols[2], cols[3])], axis=1)


def _block_diag(xg):
    lane = lax.broadcasted_iota(jnp.int32, xg.shape, 1)
    zero = jnp.zeros_like(xg)
    return jnp.concatenate(
        [jnp.where((lane >= e * SSD_HEAD_DIM) & (lane < (e + 1) * SSD_HEAD_DIM), xg, zero)
         for e in range(SSD_GROUP_HEADS)], axis=0)


def _ssd_bwd_kernel(xbc_ref, dt_ref, alog_ref, yb_ref, st_ref, *, n_heads, n_groups):
    t = SSD_CHUNK
    gh = SSD_GROUP_HEADS
    gw = gh * SSD_HEAD_DIM
    d_inner = n_heads * SSD_HEAD_DIM
    n_chunks = xbc_ref.shape[0] // t

    @pl.when(pl.program_id(1) == 0)
    def _():
        st_ref[...] = jnp.zeros_like(st_ref)

    a_row = -jnp.exp(alog_ref[...])
    ri = lax.broadcasted_iota(jnp.int32, (t, t), 0)
    ci = lax.broadcasted_iota(jnp.int32, (t, t), 1)
    tril = (ri >= ci).astype(F32)

    def chunk(i, carry):
        r0 = pl.multiple_of((n_chunks - 1 - i) * t, t)
        rows = pl.ds(r0, t)
        dt = dt_ref[rows, :]
        cum, total = _chunk_decays(dt, a_row, tril, n_heads)
        ws_t = (dt * jnp.exp(total - cum)).T
        decay = jnp.exp(total)
        for g in range(n_groups):
            h0 = n_heads + g * gh
            xg = xbc_ref[rows, g * gw:(g + 1) * gw]
            bg = xbc_ref[rows, d_inner + g * SSD_STATE:d_inner + (g + 1) * SSD_STATE]
            cg = xbc_ref[rows, d_inner + (n_groups + g) * SSD_STATE:d_inner + (n_groups + g + 1) * SSD_STATE]
            st = st_ref[g]
            yb_ref[rows, g * gw:(g + 1) * gw] = _dot(cg, st.astype(BF16))
            bt = bg.astype(F32).T
            bws = [(bt * ws_t[h0 + e:h0 + e + 1, :]).astype(BF16) for e in range(gh)]
            decay_g = _head_lanes([jnp.broadcast_to(decay[:, h0 + e:h0 + e + 1], (1, LANES)) for e in range(gh)])
            st_ref[g] = st * decay_g + _dot(jnp.concatenate(bws, axis=1), _block_diag(xg))
        return carry

    lax.fori_loop(0, n_chunks, chunk, 0, unroll=True)


def _ssd_main_kernel(xbc_ref, dt_ref, z_ref, yb_ref, x_ref, alog_ref, dexp_ref, nw_ref, wout_ref,
                     o_ref, st_ref, y_ref, yn_ref, *, n_heads, n_groups):
    t = SSD_CHUNK
    gh = SSD_GROUP_HEADS
    gw = gh * SSD_HEAD_DIM
    d_inner = n_heads * SSD_HEAD_DIM
    n_chunks = xbc_ref.shape[0] // t

    @pl.when(pl.program_id(1) == 0)
    def _():
        st_ref[...] = jnp.zeros_like(st_ref)

    a_row = -jnp.exp(alog_ref[...])
    ri = lax.broadcasted_iota(jnp.int32, (t, t), 0)
    ci = lax.broadcasted_iota(jnp.int32, (t, t), 1)
    lower = ri >= ci
    tril = lower.astype(F32)

    def chunk(i):
        rows = slice(i * t, (i + 1) * t)
        dt = dt_ref[rows, :]
        cum, total = _chunk_decays(dt, a_row, tril, n_heads)
        cum_t = (cum * LOG2E).T
        dt_t = dt.T
        ws_t = (dt * jnp.exp(total - cum)).T
        for g in range(n_groups):
            cols = slice(g * gw, (g + 1) * gw)
            xg = xbc_ref[rows, cols]
            bg = xbc_ref[rows, d_inner + g * SSD_STATE:d_inner + (g + 1) * SSD_STATE]
            cg = xbc_ref[rows, d_inner + (n_groups + g) * SSD_STATE:d_inner + (n_groups + g + 1) * SSD_STATE]
            bt = bg.astype(F32).T
            cb = _dot(cg, bt.astype(BF16)).astype(BF16)
            ms, bws, ef, eb = [], [], [], []
            for e in range(gh):
                hf = g * gh + e
                hb = n_heads + hf
                cf_col = _bcol(cum_t, hf)
                cb_col = _bcol(cum_t, hb)
                arg = jnp.where(lower, cf_col - cum_t[hf:hf + 1, :], cb_col - cum_t[hb:hb + 1, :])
                dtf = dt_t[hf:hf + 1, :]
                dtb = dt_t[hb:hb + 1, :]
                w = jnp.where(ri > ci, dtf, jnp.where(ri < ci, dtb, dtf + dtb))
                ms.append((jnp.exp2(arg) * w).astype(BF16) * cb)
                bws.append((bt * ws_t[hf:hf + 1, :]).astype(BF16))
                ef.append(cf_col)
                eb.append(cb_col)
            lhs = jnp.concatenate([jnp.concatenate(ms, axis=1), jnp.concatenate(bws, axis=1)], axis=0)
            r = _dot(lhs, _block_diag(xg))
            e_f = jnp.exp2(_head_lanes(ef))
            st = st_ref[g]
            y_ref[rows, cols] = (r[:t] + xg.astype(F32) * dexp_ref[:, cols] + _dot(cg, st.astype(BF16)) * e_f
                                 + yb_ref[rows, cols] * jnp.exp2(_head_lanes(eb)))
            st_ref[g] = st * e_f[t - 1:t, :] + r[t:]

    for i in range(n_chunks):
        chunk(i)

    half = y_ref.shape[0] // 2

    def gate_norm(r):
        rows = slice(r * half, (r + 1) * half)
        y = y_ref[rows, :] * _silu(z_ref[rows, :].astype(F32))
        yn_ref[r] = _rmsnorm(y, nw_ref[...]).astype(BF16)

    def project_out(r):
        rows = slice(r * half, (r + 1) * half)
        o_ref[rows, :] = x_ref[rows, :] + _dot(yn_ref[r], wout_ref[...])

    gate_norm(0)
    gate_norm(1)
    project_out(0)
    project_out(1)


def _ssd_scan(x, xbc, dt, z, alog, dexp, nw, wout, *, batch, seq, n_heads, n_groups):
    n, d = x.shape
    d_inner = n_heads * SSD_HEAD_DIM
    tb = min(ROW_TILE, seq)
    bps = seq // tb
    state = pltpu.VMEM((n_groups, SSD_STATE, SSD_GROUP_HEADS * SSD_HEAD_DIM), F32)
    rev = lambda w: pl.BlockSpec((tb, w), lambda b, j: (b * bps + bps - 1 - j, 0))

    yb = pl.pallas_call(
        functools.partial(_ssd_bwd_kernel, n_heads=n_heads, n_groups=n_groups),
        grid=(batch, bps),
        in_specs=[rev(xbc.shape[1]), rev(LANES), _resident(alog.shape)],
        out_specs=rev(d_inner),
        out_shape=jax.ShapeDtypeStruct((n, d_inner), F32),
        scratch_shapes=[state],
        compiler_params=_params("arbitrary", "arbitrary"),
        name="ssd_bwd",
    )(xbc, dt, alog)

    fwd = lambda w: pl.BlockSpec((tb, w), lambda b, j: (b * bps + j, 0))
    return pl.pallas_call(
        functools.partial(_ssd_main_kernel, n_heads=n_heads, n_groups=n_groups),
        grid=(batch, bps),
        in_specs=[fwd(xbc.shape[1]), fwd(LANES), fwd(d_inner), fwd(d_inner), fwd(d),
                  _resident(alog.shape), _resident(dexp.shape), _resident(nw.shape), _resident(wout.shape)],
        out_specs=fwd(d),
        out_shape=jax.ShapeDtypeStruct((n, d), F32),
        scratch_shapes=[state, pltpu.VMEM((tb, d_inner), F32), pltpu.VMEM((2, tb // 2, d_inner), BF16)],
        compiler_params=_params("arbitrary", "arbitrary"),
        name="ssd_main",
    )(xbc, dt, z, yb, x, alog, dexp, nw, wout)


def _attn_qkv_kernel(x_ref, nw_ref, wt_ref, qnt_ref, knt_ref, cost_ref, sint_ref, qt_ref, k_ref, vt_ref, y_ref):
    dh = ATTN_HEAD_DIM
    dq = qt_ref.shape[0]
    dk = vt_ref.shape[0]
    xn = _rmsnorm(x_ref[...], nw_ref[...]).astype(BF16)
    cost = cost_ref[...]
    sint = sint_ref[...]

    def norm_rope(a, w_col, scale):
        an = a * lax.rsqrt(jnp.sum(a * a, axis=0, keepdims=True) * (1.0 / dh) + EPS) * w_col
        x1, x2 = an[:dh // 2], an[dh // 2:]
        return jnp.concatenate([x1 * cost - x2 * sint, x2 * cost + x1 * sint], axis=0) * scale

    q_scale = dh ** -0.5 * LOG2E
    qnt = qnt_ref[...]
    knt = knt_ref[...]
    n_q = dq // MXU_WIDTH

    def project(c):
        rows = min(MXU_WIDTH, dq + dk - c * MXU_WIDTH)
        y_ref[c % 2, 0:rows, :] = _dot_nt(wt_ref[c * MXU_WIDTH:c * MXU_WIDTH + rows, :], xn)

    def finish_q(c):
        for hh in range(MXU_WIDTH // dh):
            qt_ref[c * MXU_WIDTH + hh * dh:c * MXU_WIDTH + (hh + 1) * dh, :] = norm_rope(
                y_ref[c % 2, hh * dh:(hh + 1) * dh, :], qnt, q_scale).astype(BF16)

    assert dq % MXU_WIDTH == 0 and dk <= MXU_WIDTH
    project(0)
    for c in range(n_q):
        project(c + 1)
        finish_q(c)
    vt_ref[...] = _dot_nt(wt_ref[dq + dk:, :], xn).astype(BF16)
    kt = jnp.concatenate(
        [norm_rope(y_ref[n_q % 2, g * dh:(g + 1) * dh, :], knt, 1.0) for g in range(dk // dh)], axis=0)
    k_ref[...] = kt.T.astype(BF16)


def _attn_core_kernel(sink_ref, qt_ref, kp_ref, k_ref, kq_ref, vtp_ref, vt_ref, vtq_ref, x_ref, woutt_ref,
                      o_ref, kbuf_ref, vbuf_ref, obuf_ref, s_ref, p_ref, inv_ref, *, blocks_per_seq, n_kv, group):
    blk = ATTN_BLOCK
    dh = ATTN_HEAD_DIM
    tq = qt_ref.shape[1]
    n_sub = tq // blk
    j = lax.rem(pl.program_id(0), blocks_per_seq)
    kbuf_ref[0:blk, :] = kp_ref[...]
    kbuf_ref[blk:blk + tq, :] = k_ref[...]
    kbuf_ref[blk + tq:, :] = kq_ref[...]
    vbuf_ref[:, 0:blk] = vtp_ref[...]
    vbuf_ref[:, blk:blk + tq] = vt_ref[...]
    vbuf_ref[:, blk + tq:] = vtq_ref[...]

    si = lax.broadcasted_iota(jnp.int32, (3 * blk, blk), 0)
    qi = lax.broadcasted_iota(jnp.int32, (3 * blk, blk), 1)
    rel = si - blk - qi
    band = (rel <= blk) & (rel >= -blk)

    def bias_of(i):
        valid = band
        if i == 0:
            valid = valid & jnp.logical_not((j == 0) & (si < blk))
        if i == n_sub - 1:
            valid = valid & jnp.logical_not((j == blocks_per_seq - 1) & (si >= 2 * blk))
        return jnp.tile(jnp.where(valid, 0.0, -jnp.inf), (1, group))

    def scores(i, g, slot):
        r0 = i * blk
        kk = kbuf_ref[r0:r0 + 3 * blk, (g // 2) * LANES:(g // 2 + 1) * LANES]
        qcat = jnp.concatenate(
            [qt_ref[(g * group + e) * dh:(g * group + e + 1) * dh, r0:r0 + blk] for e in range(group)], axis=1)
        zero = jnp.zeros_like(qcat)
        s_ref[slot] = _dot(kk, jnp.concatenate([qcat, zero] if g % 2 == 0 else [zero, qcat], axis=0))

    def softmax(bias, g, slot):
        s = s_ref[slot] + bias
        sink = jnp.concatenate(
            [jnp.full((1, blk), sink_ref[g * group + e] * LOG2E, F32) for e in range(group)], axis=1)
        m = jnp.maximum(jnp.max(s, axis=0, keepdims=True), sink)
        pr = jnp.exp2(s - m)
        inv_ref[slot] = 1.0 / (jnp.sum(pr, axis=0, keepdims=True) + jnp.exp2(sink - m))
        p_ref[slot] = pr.astype(BF16)

    def values(i, g, slot):
        r0 = i * blk
        ot = _dot(vbuf_ref[g * dh:(g + 1) * dh, r0:r0 + 3 * blk], p_ref[slot]) * inv_ref[slot]
        for e in range(group):
            obuf_ref[(g * group + e) * dh:(g * group + e + 1) * dh, r0:r0 + blk] = (
                ot[:, e * blk:(e + 1) * blk].astype(BF16))

    def project_out(half):
        rows = slice(half * tq // 2, (half + 1) * tq // 2)
        o_ref[rows, :] = x_ref[rows, :] + _dot(woutt_ref[...], obuf_ref[:, rows]).T

    items = [(i, g) for i in range(n_sub) for g in range(n_kv)]
    biases = [bias_of(i) for i in range(n_sub)]
    scores(*items[0], 0)
    for n, (i, g) in enumerate(items):
        if n + 1 < len(items):
            scores(*items[n + 1], (n + 1) % 2)
        softmax(biases[i], g, n % 2)
        if n > 0:
            values(*items[n - 1], (n - 1) % 2)
        if n == len(items) // 2:
            project_out(0)
    values(*items[-1], (len(items) - 1) % 2)
    project_out(1)


def _attention(x, nw, wt, qnt, knt, tables, sink, woutt, *, seq, n_kv):
    n, d = x.shape
    tm = min(ROW_TILE, seq)
    bps = seq // tm
    n_heads = sink.shape[0]
    dq = n_heads * ATTN_HEAD_DIM
    dk = n_kv * ATTN_HEAD_DIM
    assert n_kv % 2 == 0 and wt.shape[0] == dq + 2 * dk
    cost, sint = tables
    row = lambda w: pl.BlockSpec((tm, w), lambda i: (i, 0))
    col = lambda h: pl.BlockSpec((h, tm), lambda i: (0, i))
    post = pl.BlockSpec((ATTN_HEAD_DIM // 2, tm), lambda i: (0, lax.rem(i, bps)))
    qt, k, vt = pl.pallas_call(
        _attn_qkv_kernel,
        grid=(n // tm,),
        in_specs=[row(d), _resident((1, d)), _resident(wt.shape), _resident(qnt.shape), _resident(knt.shape),
                  post, post],
        out_specs=[col(dq), row(dk), col(dk)],
        out_shape=[jax.ShapeDtypeStruct((dq, n), BF16), jax.ShapeDtypeStruct((n, dk), BF16),
                   jax.ShapeDtypeStruct((dk, n), BF16)],
        scratch_shapes=[pltpu.VMEM((2, MXU_WIDTH, tm), F32)],
        compiler_params=_params("arbitrary"),
        name="attn_qkv",
    )(x, nw, wt, qnt, knt, cost, sint)

    sb = tm // ATTN_BLOCK
    last = n // ATTN_BLOCK - 1
    before = lambda i: jnp.maximum(i * sb - 1, 0)
    after = lambda i: jnp.minimum((i + 1) * sb, last)
    return pl.pallas_call(
        functools.partial(_attn_core_kernel, blocks_per_seq=bps, n_kv=n_kv, group=n_heads // n_kv),
        grid=(n // tm,),
        in_specs=[pl.BlockSpec(memory_space=pltpu.SMEM), col(dq),
                  pl.BlockSpec((ATTN_BLOCK, dk), lambda i: (before(i), 0)), row(dk),
                  pl.BlockSpec((ATTN_BLOCK, dk), lambda i: (after(i), 0)),
                  pl.BlockSpec((dk, ATTN_BLOCK), lambda i: (0, before(i))), col(dk),
                  pl.BlockSpec((dk, ATTN_BLOCK), lambda i: (0, after(i))),
                  row(d), _resident(woutt.shape)],
        out_specs=row(d),
        out_shape=jax.ShapeDtypeStruct((n, d), F32),
        scratch_shapes=[pltpu.VMEM((tm + 2 * ATTN_BLOCK, dk), BF16), pltpu.VMEM((dk, tm + 2 * ATTN_BLOCK), BF16),
                        pltpu.VMEM((dq, tm), BF16),
                        pltpu.VMEM((2, 3 * ATTN_BLOCK, n_heads // n_kv * ATTN_BLOCK), F32),
                        pltpu.VMEM((2, 3 * ATTN_BLOCK, n_heads // n_kv * ATTN_BLOCK), BF16),
                        pltpu.VMEM((2, 1, n_heads // n_kv * ATTN_BLOCK), F32)],
        compiler_params=_params("arbitrary"),
        name="attn_core",
    )(sink, qt, k, k, k, vt, vt, vt, x, woutt)


def _rope_tables(seq):
    half = ATTN_HEAD_DIM // 2
    inv_freq = ROPE_THETA ** (-jnp.arange(half, dtype=F32) / half)
    ang = inv_freq[:, None] * jnp.arange(seq, dtype=jnp.int32).astype(F32)[None, :]
    return jnp.cos(ang), jnp.sin(ang)


def kernel(x, norm_w, ffn_w_gate, ffn_w_up, ffn_w_down, ssd_w_in, ssd_conv_w, ssd_conv_b, ssd_dt_bias,
           ssd_a_log, ssd_d, ssd_norm_w, ssd_w_out, attn_w_qkv, attn_q_norm, attn_k_norm, attn_sink,
           attn_w_out):
    batch, seq, d = x.shape
    depth = norm_w.shape[0]
    n_ssd_heads = ssd_d.shape[1]
    d_inner = n_ssd_heads * SSD_HEAD_DIM
    conv_ch = ssd_conv_w.shape[2]
    n_groups = (conv_ch - d_inner) // (2 * SSD_STATE)
    n_kv = (attn_w_qkv.shape[2] // ATTN_HEAD_DIM - attn_sink.shape[1]) // 2
    bf = lambda w: w.astype(BF16)
    pad_lanes = lambda v: jnp.pad(v.astype(F32).reshape(1, -1), ((0, 0), (0, LANES - v.size)))
    tables = _rope_tables(seq)

    wg, wu, wd = bf(ffn_w_gate), bf(ffn_w_up), bf(ffn_w_down)
    h = x.reshape(batch * seq, d)
    for i in range(depth):
        h = _ffn(h, norm_w[i, 0][None], wg, wu, wd, i, 0)
        nw = norm_w[i, 1][None]
        j = i // 2
        if i % 2 == 0:
            z, xbc, dt = _ssd_in(h, nw, bf(ssd_w_in[j]), ssd_conv_w[j], ssd_conv_b[j][None],
                                 pad_lanes(ssd_dt_bias[j]), seq=seq, d_z=d_inner)
            h = _ssd_scan(h, xbc, dt, z, pad_lanes(ssd_a_log[j]),
                          jnp.repeat(ssd_d[j], SSD_HEAD_DIM)[None], ssd_norm_w[j][None], bf(ssd_w_out[j]),
                          batch=batch, seq=seq, n_heads=n_ssd_heads, n_groups=n_groups)
        else:
            h = _attention(h, nw, bf(attn_w_qkv[j].T), attn_q_norm[j][:, None], attn_k_norm[j][:, None], tables,
                           attn_sink[j], bf(attn_w_out[j].T), seq=seq, n_kv=n_kv)
        h = _ffn(h, norm_w[i, 2][None], wg, wu, wd, i, 1)
    return h.reshape(batch, seq, d)
```

```python
import functools

import jax
import jax.numpy as jnp
from jax import lax
from jax.experimental import pallas as pl
from jax.experimental.pallas import tpu as pltpu

F32 = jnp.float32
BF16 = jnp.bfloat16
EPS = 1e-6
ROPE_THETA = 10000.0

V7X_VMEM_LIMIT_BYTES = 56 * 1024 * 1024
LANES = 128
HALO = 8

SSD_CHUNK = 128
SSD_HEAD_DIM = 64
SSD_STATE = 128
SSD_GROUP_HEADS = 4
SSD_CONV = 5
ATTN_HEAD_DIM = 64
ATTN_BLOCK = 128

ROW_TILE = 512
SSD_BWD_ROW_TILE = 1024
ATTN_QKV_ROW_TILE = 1024
FFN_ROW_TILE = 1024
MXU_WIDTH = 256
CONV_PHASES = 4
LOG2E = 1.4426950408889634


def _params(*semantics):
    return pltpu.CompilerParams(dimension_semantics=semantics, vmem_limit_bytes=V7X_VMEM_LIMIT_BYTES)


def _resident(shape):
    return pl.BlockSpec(shape, lambda *_: (0,) * len(shape), pipeline_mode=pl.Buffered(1))


def _rmsnorm(x, w):
    return x * lax.rsqrt(jnp.mean(x * x, axis=-1, keepdims=True) + EPS) * w


def _silu(x):
    h = 0.5 * x
    return h + h * jnp.tanh(h)


def _dot(a, b):
    return jnp.dot(a, b, preferred_element_type=F32)


def _dot_nt(a, b):
    return lax.dot_general(a, b, (((1,), (1,)), ((), ())), preferred_element_type=F32)


def _ffn_kernel(x_ref, nw_ref, wg_ref, wu_ref, wd_ref, o_ref, h_ref, *, ff_chunk):
    x = x_ref[...]
    xn = _rmsnorm(x, nw_ref[...]).astype(BF16)
    for c in range(wg_ref.shape[1] // ff_chunk):
        sl = slice(c * ff_chunk, (c + 1) * ff_chunk)
        g = _dot(xn, wg_ref[:, sl])
        u = _dot(xn, wu_ref[:, sl])
        h_ref[:, sl] = (_silu(g) * u).astype(BF16)
    o_ref[...] = x + 0.5 * _dot(h_ref[...], wd_ref[...])


def _ffn(x, nw, wg, wu, wd, layer, half):
    n, d = x.shape
    d_ff = wg.shape[-1]
    tm = min(FFN_ROW_TILE, n)
    row = pl.BlockSpec((tm, d), lambda i: (i, 0))
    pick = lambda r, c: pl.BlockSpec((None, None, r, c), lambda i: (layer, half, 0, 0),
                                     pipeline_mode=pl.Buffered(1))
    return pl.pallas_call(
        functools.partial(_ffn_kernel, ff_chunk=MXU_WIDTH),
        grid=(n // tm,),
        in_specs=[row, _resident((1, d)), pick(d, d_ff), pick(d, d_ff), pick(d_ff, d)],
        out_specs=row,
        out_shape=jax.ShapeDtypeStruct((n, d), F32),
        scratch_shapes=[pltpu.VMEM((tm, d_ff), BF16)],
        compiler_params=_params("arbitrary"),
        name="ffn",
    )(x, nw, wg, wu, wd)


def _ssd_in_kernel(xp_ref, x_ref, xq_ref, nw_ref, w_ref, cw_ref, cb_ref, dtb_ref,
                   z_ref, xbc_ref, dt_ref, p_ref, a_ref, *, blocks_per_seq, col_chunk):
    tm = x_ref.shape[0]
    d_z = z_ref.shape[1]
    d_xbc = xbc_ref.shape[1]
    j = lax.rem(pl.program_id(0), blocks_per_seq)
    nw = nw_ref[...]
    xm = _rmsnorm(x_ref[...], nw)
    xp = _rmsnorm(xp_ref[...], nw) * (j != 0).astype(F32)
    xq = _rmsnorm(xq_ref[...], nw) * (j != blocks_per_seq - 1).astype(F32)
    xm_bf = xm.astype(BF16)
    xe_bf = jnp.concatenate([xp, xm, xq], axis=0).astype(BF16)

    slabs = col_chunk // LANES
    n_chunks = d_xbc // col_chunk

    def project(c):
        u = _dot(xe_bf, w_ref[:, d_z + c * col_chunk:d_z + (c + 1) * col_chunk])
        for s in range(slabs):
            p_ref[c % 2, s] = u[:, s * LANES:(s + 1) * LANES]

    def conv(c):
        for s in range(slabs):
            lanes = slice(c * col_chunk + s * LANES, c * col_chunk + (s + 1) * LANES)
            for ph in range(CONV_PHASES):
                a = cb_ref[:, lanes]
                for k in range(SSD_CONV):
                    a = a + cw_ref[k:k + 1, lanes] * p_ref[
                        c % 2, s, pl.ds(ph + HALO - SSD_CONV // 2 + k, tm // CONV_PHASES, stride=CONV_PHASES), :]
                a_ref[s, pl.ds(ph, tm // CONV_PHASES, stride=CONV_PHASES), :] = _silu(a)
            xbc_ref[:, lanes] = a_ref[s].astype(BF16)

    def project_z(c):
        sl = slice(c * col_chunk, (c + 1) * col_chunk)
        z_ref[:, sl] = _dot(xm_bf, w_ref[:, sl]).astype(BF16)

    z_chunks = list(range(d_z // col_chunk))
    project(0)
    for c in range(n_chunks):
        if c + 1 < n_chunks:
            project(c + 1)
        if c % 2 == 1 and z_chunks:
            project_z(z_chunks.pop(0))
        conv(c)
    dt_raw = _dot(xm_bf, w_ref[:, d_z + d_xbc:])
    dt_raw = jnp.concatenate([dt_raw, jnp.zeros((tm, LANES - dt_raw.shape[1]), F32)], axis=1)
    dt_ref[...] = jax.nn.softplus(dt_raw + dtb_ref[...])
    for c in z_chunks:
        project_z(c)


def _ssd_in(x, nw, w_in, cw, cb, dtb, *, seq, d_z):
    n, d = x.shape
    d_xbc = cw.shape[1]
    tm = min(ROW_TILE, seq)
    bps = seq // tm
    hb = tm // HALO
    last_halo = n // HALO - 1
    row = lambda w: pl.BlockSpec((tm, w), lambda i: (i, 0))
    prev = pl.BlockSpec((HALO, d), lambda i: (jnp.maximum(i * hb - 1, 0), 0))
    nxt = pl.BlockSpec((HALO, d), lambda i: (jnp.minimum((i + 1) * hb, last_halo), 0))
    col_chunk = 2 * MXU_WIDTH
    return pl.pallas_call(
        functools.partial(_ssd_in_kernel, blocks_per_seq=bps, col_chunk=col_chunk),
        grid=(n // tm,),
        in_specs=[prev, row(d), nxt, _resident((1, d)), _resident(w_in.shape),
                  _resident(cw.shape), _resident(cb.shape), _resident(dtb.shape)],
        out_specs=[row(d_z), row(d_xbc), row(LANES)],
        out_shape=[jax.ShapeDtypeStruct((n, d_z), BF16),
                   jax.ShapeDtypeStruct((n, d_xbc), BF16),
                   jax.ShapeDtypeStruct((n, LANES), F32)],
        scratch_shapes=[pltpu.VMEM((2, col_chunk // LANES, tm + 2 * HALO, LANES), F32),
                        pltpu.VMEM((col_chunk // LANES, tm, LANES), F32)],
        compiler_params=_params("arbitrary"),
        name="ssd_in",
    )(x, x, x, nw, w_in, cw, cb, dtb)


def _chunk_decays(dt, a_row, tril, n_heads):
    t = dt.shape[0]
    da = dt * a_row
    prefix = jnp.dot(tril, da, precision=lax.Precision.HIGHEST, preferred_element_type=F32)
    total = prefix[t - 1:t, :]
    lane = lax.broadcasted_iota(jnp.int32, dt.shape, 1)
    return jnp.where(lane < n_heads, prefix, total - prefix + da), total


def _bcol(v_t, j):
    return jnp.broadcast_to(v_t[j:j + 1, :], (LANES, v_t.shape[1])).T


def _head_lanes(cols):
    first = lax.broadcasted_iota(jnp.int32, cols[0].shape, 1) < SSD_HEAD_DIM
    return jnp.concatenate([jnp.where(first, cols[0], cols[1]), jnp.where(first, cols[2], cols[3])], axis=1)


def _block_diag(xg):
    lane = lax.broadcasted_iota(jnp.int32, xg.shape, 1)
    zero = jnp.zeros_like(xg)
    return jnp.concatenate(
        [jnp.where((lane >= e * SSD_HEAD_DIM) & (lane < (e + 1) * SSD_HEAD_DIM), xg, zero)
         for e in range(SSD_GROUP_HEADS)], axis=0)


def _ssd_bwd_kernel(xbc_ref, dt_ref, alog_ref, yb_ref, st_ref, *, n_heads, n_groups):
    t = SSD_CHUNK
    gh = SSD_GROUP_HEADS
    gw = gh * SSD_HEAD_DIM
    d_inner = n_heads * SSD_HEAD_DIM
    n_chunks = xbc_ref.shape[0] // t

    @pl.when(pl.program_id(1) == 0)
    def _():
        st_ref[...] = jnp.zeros_like(st_ref)

    a_row = -jnp.exp(alog_ref[...])
    ri = lax.broadcasted_iota(jnp.int32, (t, t), 0)
    ci = lax.broadcasted_iota(jnp.int32, (t, t), 1)
    tril = (ri >= ci).astype(F32)

    def chunk(i, carry):
        r0 = pl.multiple_of((n_chunks - 1 - i) * t, t)
        rows = pl.ds(r0, t)
        dt = dt_ref[rows, :]
        cum, total = _chunk_decays(dt, a_row, tril, n_heads)
        ws_t = (dt * jnp.exp(total - cum)).T
        decay = jnp.exp(total)
        for g in range(n_groups):
            h0 = n_heads + g * gh
            xg = xbc_ref[rows, g * gw:(g + 1) * gw]
            bg = xbc_ref[rows, d_inner + g * SSD_STATE:d_inner + (g + 1) * SSD_STATE]
            cg = xbc_ref[rows, d_inner + (n_groups + g) * SSD_STATE:d_inner + (n_groups + g + 1) * SSD_STATE]
            st = st_ref[g]
            yb_ref[rows, g * gw:(g + 1) * gw] = _dot(cg, st.astype(BF16)).astype(BF16)
            bt = bg.astype(F32).T
            bws = [(bt * ws_t[h0 + e:h0 + e + 1, :]).astype(BF16) for e in range(gh)]
            decay_g = _head_lanes([jnp.broadcast_to(decay[:, h0 + e:h0 + e + 1], (1, LANES)) for e in range(gh)])
            st_ref[g] = st * decay_g + _dot(jnp.concatenate(bws, axis=1), _block_diag(xg))
        return carry

    lax.fori_loop(0, n_chunks, chunk, 0, unroll=True)


def _ssd_main_kernel(xbc_ref, dt_ref, z_ref, yb_ref, x_ref, alog_ref, dexp_ref, nw_ref, wout_ref,
                     o_ref, st_ref, y_ref, yn_ref, *, n_heads, n_groups):
    t = SSD_CHUNK
    gh = SSD_GROUP_HEADS
    gw = gh * SSD_HEAD_DIM
    d_inner = n_heads * SSD_HEAD_DIM
    n_chunks = xbc_ref.shape[0] // t

    @pl.when(pl.program_id(1) == 0)
    def _():
        st_ref[...] = jnp.zeros_like(st_ref)

    a_row = -jnp.exp(alog_ref[...])
    ri = lax.broadcasted_iota(jnp.int32, (t, t), 0)
    ci = lax.broadcasted_iota(jnp.int32, (t, t), 1)
    lower = ri >= ci
    tril = lower.astype(F32)

    def chunk(i):
        rows = slice(i * t, (i + 1) * t)
        dt = dt_ref[rows, :]
        cum, total = _chunk_decays(dt, a_row, tril, n_heads)
        cum_t = (cum * LOG2E).T
        dt_t = dt.T
        ws_t = (dt * jnp.exp(total - cum)).T
        for g in range(n_groups):
            cols = slice(g * gw, (g + 1) * gw)
            xg = xbc_ref[rows, cols]
            bg = xbc_ref[rows, d_inner + g * SSD_STATE:d_inner + (g + 1) * SSD_STATE]
            cg = xbc_ref[rows, d_inner + (n_groups + g) * SSD_STATE:d_inner + (n_groups + g + 1) * SSD_STATE]
            bt = bg.astype(F32).T
            cb = _dot(cg, bt.astype(BF16)).astype(BF16)
            ms, bws, ef, eb = [], [], [], []
            for e in range(gh):
                hf = g * gh + e
                hb = n_heads + hf
                cf_col = _bcol(cum_t, hf)
                cb_col = _bcol(cum_t, hb)
                arg = jnp.where(lower, cf_col - cum_t[hf:hf + 1, :], cb_col - cum_t[hb:hb + 1, :])
                dtf = dt_t[hf:hf + 1, :]
                dtb = dt_t[hb:hb + 1, :]
                w = jnp.where(ri > ci, dtf, jnp.where(ri < ci, dtb, dtf + dtb))
                ms.append((jnp.exp2(arg) * w).astype(BF16) * cb)
                bws.append((bt * ws_t[hf:hf + 1, :]).astype(BF16))
                ef.append(cf_col)
                eb.append(cb_col)
            lhs = jnp.concatenate([jnp.concatenate(ms, axis=1), jnp.concatenate(bws, axis=1)], axis=0)
            r = _dot(lhs, _block_diag(xg))
            e_f = jnp.exp2(_head_lanes(ef))
            st = st_ref[g]
            y_ref[rows, cols] = (r[:t] + xg.astype(F32) * dexp_ref[:, cols] + _dot(cg, st.astype(BF16)) * e_f
                                 + yb_ref[rows, cols].astype(F32) * jnp.exp2(_head_lanes(eb)))
            st_ref[g] = st * e_f[t - 1:t, :] + r[t:]

    for i in range(n_chunks):
        chunk(i)

    half = y_ref.shape[0] // 2

    def gate_norm(r):
        rows = slice(r * half, (r + 1) * half)
        y = y_ref[rows, :] * _silu(z_ref[rows, :].astype(F32))
        yn_ref[r] = _rmsnorm(y, nw_ref[...]).astype(BF16)

    def project_out(r):
        rows = slice(r * half, (r + 1) * half)
        o_ref[rows, :] = x_ref[rows, :] + _dot(yn_ref[r], wout_ref[...])

    gate_norm(0)
    project_out(0)
    gate_norm(1)
    project_out(1)


def _ssd_scan(x, xbc, dt, z, alog, dexp, nw, wout, *, batch, seq, n_heads, n_groups):
    n, d = x.shape
    d_inner = n_heads * SSD_HEAD_DIM
    state = pltpu.VMEM((n_groups, SSD_STATE, SSD_GROUP_HEADS * SSD_HEAD_DIM), F32)
    tr = min(SSD_BWD_ROW_TILE, seq)
    rps = seq // tr
    rev = lambda w: pl.BlockSpec((tr, w), lambda b, j: (b * rps + rps - 1 - j, 0))

    yb = pl.pallas_call(
        functools.partial(_ssd_bwd_kernel, n_heads=n_heads, n_groups=n_groups),
        grid=(batch, rps),
        in_specs=[rev(xbc.shape[1]), rev(LANES), _resident(alog.shape)],
        out_specs=rev(d_inner),
        out_shape=jax.ShapeDtypeStruct((n, d_inner), BF16),
        scratch_shapes=[state],
        compiler_params=_params("arbitrary", "arbitrary"),
        name="ssd_bwd",
    )(xbc, dt, alog)

    tb = min(ROW_TILE, seq)
    bps = seq // tb
    fwd = lambda w: pl.BlockSpec((tb, w), lambda b, j: (b * bps + j, 0))
    return pl.pallas_call(
        functools.partial(_ssd_main_kernel, n_heads=n_heads, n_groups=n_groups),
        grid=(batch, bps),
        in_specs=[fwd(xbc.shape[1]), fwd(LANES), fwd(d_inner), fwd(d_inner), fwd(d),
                  _resident(alog.shape), _resident(dexp.shape), _resident(nw.shape), _resident(wout.shape)],
        out_specs=fwd(d),
        out_shape=jax.ShapeDtypeStruct((n, d), F32),
        scratch_shapes=[state, pltpu.VMEM((tb, d_inner), F32), pltpu.VMEM((2, tb // 2, d_inner), BF16)],
        compiler_params=_params("arbitrary", "arbitrary"),
        name="ssd_main",
    )(xbc, dt, z, yb, x, alog, dexp, nw, wout)


def _attn_qkv_kernel(x_ref, nw_ref, wt_ref, qnt_ref, knt_ref, cost_ref, sint_ref, qt_ref, k_ref, vt_ref, y_ref):
    dh = ATTN_HEAD_DIM
    dq = qt_ref.shape[0]
    dk = vt_ref.shape[0]
    xn = _rmsnorm(x_ref[...], nw_ref[...]).astype(BF16)
    cost = cost_ref[...]
    sint = sint_ref[...]

    def norm_rope(a, w_col, scale):
        an = a * lax.rsqrt(jnp.sum(a * a, axis=0, keepdims=True) * (1.0 / dh) + EPS) * w_col
        x1, x2 = an[:dh // 2], an[dh // 2:]
        return jnp.concatenate([x1 * cost - x2 * sint, x2 * cost + x1 * sint], axis=0) * scale

    q_scale = dh ** -0.5 * LOG2E
    qnt = qnt_ref[...]
    knt = knt_ref[...]
    n_q = dq // MXU_WIDTH

    def project(c):
        rows = min(MXU_WIDTH, dq + dk - c * MXU_WIDTH)
        y_ref[c % 2, 0:rows, :] = _dot_nt(wt_ref[c * MXU_WIDTH:c * MXU_WIDTH + rows, :], xn)

    def finish_q(c):
        for hh in range(MXU_WIDTH // dh):
            qt_ref[c * MXU_WIDTH + hh * dh:c * MXU_WIDTH + (hh + 1) * dh, :] = norm_rope(
                y_ref[c % 2, hh * dh:(hh + 1) * dh, :], qnt, q_scale).astype(BF16)

    assert dq % MXU_WIDTH == 0 and dk <= MXU_WIDTH
    project(0)
    for c in range(n_q):
        project(c + 1)
        finish_q(c)
    vt_ref[...] = _dot_nt(wt_ref[dq + dk:, :], xn).astype(BF16)
    kt = jnp.concatenate(
        [norm_rope(y_ref[n_q % 2, g * dh:(g + 1) * dh, :], knt, 1.0) for g in range(dk // dh)], axis=0)
    k_ref[...] = kt.T.astype(BF16)


def _attn_core_kernel(sink_ref, qt_ref, kp_ref, k_ref, kq_ref, vtp_ref, vt_ref, vtq_ref, x_ref, woutt_ref,
                      o_ref, kbuf_ref, vbuf_ref, obuf_ref, s_ref, p_ref, inv_ref, *, blocks_per_seq, n_kv, group):
    blk = ATTN_BLOCK
    dh = ATTN_HEAD_DIM
    tq = qt_ref.shape[1]
    n_sub = tq // blk
    j = lax.rem(pl.program_id(0), blocks_per_seq)
    kbuf_ref[0:blk, :] = kp_ref[...]
    kbuf_ref[blk:blk + tq, :] = k_ref[...]
    kbuf_ref[blk + tq:, :] = kq_ref[...]
    vbuf_ref[:, 0:blk] = vtp_ref[...]
    vbuf_ref[:, blk:blk + tq] = vt_ref[...]
    vbuf_ref[:, blk + tq:] = vtq_ref[...]

    si = lax.broadcasted_iota(jnp.int32, (3 * blk, blk), 0)
    qi = lax.broadcasted_iota(jnp.int32, (3 * blk, blk), 1)
    rel = si - blk - qi
    band = (rel <= blk) & (rel >= -blk)

    def bias_of(i):
        valid = band
        if i == 0:
            valid = valid & jnp.logical_not((j == 0) & (si < blk))
        if i == n_sub - 1:
            valid = valid & jnp.logical_not((j == blocks_per_seq - 1) & (si >= 2 * blk))
        return jnp.tile(jnp.where(valid, 0.0, -jnp.inf), (1, group))

    def scores(i, g, slot):
        r0 = i * blk
        kk = kbuf_ref[r0:r0 + 3 * blk, (g // 2) * LANES:(g // 2 + 1) * LANES]
        qcat = jnp.concatenate(
            [qt_ref[(g * group + e) * dh:(g * group + e + 1) * dh, r0:r0 + blk] for e in range(group)], axis=1)
        zero = jnp.zeros_like(qcat)
        s_ref[slot] = _dot(kk, jnp.concatenate([qcat, zero] if g % 2 == 0 else [zero, qcat], axis=0))

    def softmax(bias, g, slot):
        s = s_ref[slot] + bias
        sink = jnp.concatenate(
            [jnp.full((1, blk), sink_ref[g * group + e] * LOG2E, F32) for e in range(group)], axis=1)
        m = jnp.maximum(jnp.max(s, axis=0, keepdims=True), sink)
        pr = jnp.exp2(s - m)
        inv_ref[slot] = 1.0 / (jnp.sum(pr, axis=0, keepdims=True) + jnp.exp2(sink - m))
        p_ref[slot] = pr.astype(BF16)

    def values(i, g, slot):
        r0 = i * blk
        ot = _dot(vbuf_ref[g * dh:(g + 1) * dh, r0:r0 + 3 * blk], p_ref[slot]) * inv_ref[slot]
        for e in range(group):
            obuf_ref[(g * group + e) * dh:(g * group + e + 1) * dh, r0:r0 + blk] = (
                ot[:, e * blk:(e + 1) * blk].astype(BF16))

    def project_out(half):
        rows = slice(half * tq // 2, (half + 1) * tq // 2)
        o_ref[rows, :] = x_ref[rows, :] + _dot(woutt_ref[...], obuf_ref[:, rows]).T

    items = [(i, g) for i in range(n_sub) for g in range(n_kv)]
    biases = [bias_of(i) for i in range(n_sub)]
    scores(*items[0], 0)
    for n, (i, g) in enumerate(items):
        if n + 1 < len(items):
            scores(*items[n + 1], (n + 1) % 2)
        softmax(biases[i], g, n % 2)
        if n > 0:
            values(*items[n - 1], (n - 1) % 2)
        if n == len(items) // 2:
            project_out(0)
    values(*items[-1], (len(items) - 1) % 2)
    project_out(1)


def _attention(x, nw, wt, qnt, knt, tables, sink, woutt, *, seq, n_kv):
    n, d = x.shape
    n_heads = sink.shape[0]
    dq = n_heads * ATTN_HEAD_DIM
    dk = n_kv * ATTN_HEAD_DIM
    assert n_kv % 2 == 0 and wt.shape[0] == dq + 2 * dk
    cost, sint = tables
    row = lambda t, w: pl.BlockSpec((t, w), lambda i: (i, 0))
    col = lambda t, h: pl.BlockSpec((h, t), lambda i: (0, i))

    tm = min(ATTN_QKV_ROW_TILE, seq)
    post = pl.BlockSpec((ATTN_HEAD_DIM // 2, tm), lambda i: (0, lax.rem(i, seq // tm)))
    qt, k, vt = pl.pallas_call(
        _attn_qkv_kernel,
        grid=(n // tm,),
        in_specs=[row(tm, d), _resident((1, d)), _resident(wt.shape), _resident(qnt.shape),
                  _resident(knt.shape), post, post],
        out_specs=[col(tm, dq), row(tm, dk), col(tm, dk)],
        out_shape=[jax.ShapeDtypeStruct((dq, n), BF16), jax.ShapeDtypeStruct((n, dk), BF16),
                   jax.ShapeDtypeStruct((dk, n), BF16)],
        scratch_shapes=[pltpu.VMEM((2, MXU_WIDTH, tm), F32)],
        compiler_params=_params("arbitrary"),
        name="attn_qkv",
    )(x, nw, wt, qnt, knt, cost, sint)

    tc = min(ROW_TILE, seq)
    sb = tc // ATTN_BLOCK
    last = n // ATTN_BLOCK - 1
    before = lambda i: jnp.maximum(i * sb - 1, 0)
    after = lambda i: jnp.minimum((i + 1) * sb, last)
    return pl.pallas_call(
        functools.partial(_attn_core_kernel, blocks_per_seq=seq // tc, n_kv=n_kv, group=n_heads // n_kv),
        grid=(n // tc,),
        in_specs=[pl.BlockSpec(memory_space=pltpu.SMEM), col(tc, dq),
                  pl.BlockSpec((ATTN_BLOCK, dk), lambda i: (before(i), 0)), row(tc, dk),
                  pl.BlockSpec((ATTN_BLOCK, dk), lambda i: (after(i), 0)),
                  pl.BlockSpec((dk, ATTN_BLOCK), lambda i: (0, before(i))), col(tc, dk),
                  pl.BlockSpec((dk, ATTN_BLOCK), lambda i: (0, after(i))),
                  row(tc, d), _resident(woutt.shape)],
        out_specs=row(tc, d),
        out_shape=jax.ShapeDtypeStruct((n, d), F32),
        scratch_shapes=[pltpu.VMEM((tc + 2 * ATTN_BLOCK, dk), BF16), pltpu.VMEM((dk, tc + 2 * ATTN_BLOCK), BF16),
                        pltpu.VMEM((dq, tc), BF16),
                        pltpu.VMEM((2, 3 * ATTN_BLOCK, n_heads // n_kv * ATTN_BLOCK), F32),
                        pltpu.VMEM((2, 3 * ATTN_BLOCK, n_heads // n_kv * ATTN_BLOCK), BF16),
                        pltpu.VMEM((2, 1, n_heads // n_kv * ATTN_BLOCK), F32)],
        compiler_params=_params("arbitrary"),
        name="attn_core",
    )(sink, qt, k, k, k, vt, vt, vt, x, woutt)


def _rope_tables(seq):
    half = ATTN_HEAD_DIM // 2
    inv_freq = ROPE_THETA ** (-jnp.arange(half, dtype=F32) / half)
    ang = inv_freq[:, None] * jnp.arange(seq, dtype=jnp.int32).astype(F32)[None, :]
    return jnp.cos(ang), jnp.sin(ang)


def kernel(x, norm_w, ffn_w_gate, ffn_w_up, ffn_w_down, ssd_w_in, ssd_conv_w, ssd_conv_b, ssd_dt_bias,
           ssd_a_log, ssd_d, ssd_norm_w, ssd_w_out, attn_w_qkv, attn_q_norm, attn_k_norm, attn_sink,
           attn_w_out):
    batch, seq, d = x.shape
    depth = norm_w.shape[0]
    n_ssd_heads = ssd_d.shape[1]
    d_inner = n_ssd_heads * SSD_HEAD_DIM
    conv_ch = ssd_conv_w.shape[2]
    n_groups = (conv_ch - d_inner) // (2 * SSD_STATE)
    n_kv = (attn_w_qkv.shape[2] // ATTN_HEAD_DIM - attn_sink.shape[1]) // 2
    bf = lambda w: w.astype(BF16)
    pad_lanes = lambda v: jnp.pad(v.astype(F32).reshape(1, -1), ((0, 0), (0, LANES - v.size)))
    tables = _rope_tables(seq)

    wg, wu, wd = bf(ffn_w_gate), bf(ffn_w_up), bf(ffn_w_down)
    h = x.reshape(batch * seq, d)
    for i in range(depth):
        h = _ffn(h, norm_w[i, 0][None], wg, wu, wd, i, 0)
        nw = norm_w[i, 1][None]
        j = i // 2
        if i % 2 == 0:
            z, xbc, dt = _ssd_in(h, nw, bf(ssd_w_in[j]), ssd_conv_w[j], ssd_conv_b[j][None],
                                 pad_lanes(ssd_dt_bias[j]), seq=seq, d_z=d_inner)
            h = _ssd_scan(h, xbc, dt, z, pad_lanes(ssd_a_log[j]),
                          jnp.repeat(ssd_d[j], SSD_HEAD_DIM)[None], ssd_norm_w[j][None], bf(ssd_w_out[j]),
                          batch=batch, seq=seq, n_heads=n_ssd_heads, n_groups=n_groups)
        else:
            h = _attention(h, nw, bf(attn_w_qkv[j].T), attn_q_norm[j][:, None], attn_k_norm[j][:, None], tables,
                           attn_sink[j], bf(attn_w_out[j].T), seq=seq, n_kv=n_kv)
        h = _ffn(h, norm_w[i, 2][None], wg, wu, wd, i, 1)
    return h.reshape(batch, seq, d)
```

```python
import functools

import jax
import jax.numpy as jnp
from jax import lax
from jax.experimental import pallas as pl
from jax.experimental.pallas import tpu as pltpu

F32 = jnp.float32
BF16 = jnp.bfloat16
EPS = 1e-6
ROPE_THETA = 10000.0

V7X_VMEM_LIMIT_BYTES = 56 * 1024 * 1024
LANES = 128
HALO = 8

SSD_CHUNK = 128
SSD_HEAD_DIM = 64
SSD_STATE = 128
SSD_GROUP_HEADS = 4
SSD_CONV = 5
ATTN_HEAD_DIM = 64
ATTN_BLOCK = 128

ROW_TILE = 512
SSD_BWD_ROW_TILE = 1024
ATTN_QKV_ROW_TILE = 1024
FFN_ROW_TILE = 1024
MXU_WIDTH = 256
CONV_PHASES = 4
LOG2E = 1.4426950408889634


def _params(*semantics):
    return pltpu.CompilerParams(dimension_semantics=semantics, vmem_limit_bytes=V7X_VMEM_LIMIT_BYTES)


def _resident(shape):
    return pl.BlockSpec(shape, lambda *_: (0,) * len(shape), pipeline_mode=pl.Buffered(1))


def _rmsnorm(x, w):
    return x * lax.rsqrt(jnp.mean(x * x, axis=-1, keepdims=True) + EPS) * w


def _silu(x):
    h = 0.5 * x
    return h + h * jnp.tanh(h)


def _dot(a, b):
    return jnp.dot(a, b, preferred_element_type=F32)


def _dot_nt(a, b):
    return lax.dot_general(a, b, (((1,), (1,)), ((), ())), preferred_element_type=F32)


def _ffn_kernel(x_ref, nw_ref, wg_ref, wu_ref, wd_ref, o_ref, h_ref, *, ff_chunk):
    x = x_ref[...]
    xn = _rmsnorm(x, nw_ref[...]).astype(BF16)
    for c in range(wg_ref.shape[1] // ff_chunk):
        sl = slice(c * ff_chunk, (c + 1) * ff_chunk)
        g = _dot(xn, wg_ref[:, sl])
        u = _dot(xn, wu_ref[:, sl])
        h_ref[:, sl] = (_silu(g) * u).astype(BF16)
    o_ref[...] = x + 0.5 * _dot(h_ref[...], wd_ref[...])


def _ffn(x, nw, wg, wu, wd, layer, half):
    n, d = x.shape
    d_ff = wg.shape[-1]
    tm = min(FFN_ROW_TILE, n)
    row = pl.BlockSpec((tm, d), lambda i: (i, 0))
    pick = lambda r, c: pl.BlockSpec((None, None, r, c), lambda i: (layer, half, 0, 0),
                                     pipeline_mode=pl.Buffered(1))
    return pl.pallas_call(
        functools.partial(_ffn_kernel, ff_chunk=MXU_WIDTH),
        grid=(n // tm,),
        in_specs=[row, _resident((1, d)), pick(d, d_ff), pick(d, d_ff), pick(d_ff, d)],
        out_specs=row,
        out_shape=jax.ShapeDtypeStruct((n, d), F32),
        scratch_shapes=[pltpu.VMEM((tm, d_ff), BF16)],
        compiler_params=_params("arbitrary"),
        name="ffn",
    )(x, nw, wg, wu, wd)


def _ssd_in_kernel(xp_ref, x_ref, xq_ref, nw_ref, w_ref, cw_ref, cb_ref, dtb_ref,
                   z_ref, xbc_ref, dt_ref, p_ref, a_ref, *, blocks_per_seq, col_chunk):
    tm = x_ref.shape[0]
    d_z = z_ref.shape[1]
    d_xbc = xbc_ref.shape[1]
    j = lax.rem(pl.program_id(0), blocks_per_seq)
    nw = nw_ref[...]
    xm = _rmsnorm(x_ref[...], nw)
    xp = _rmsnorm(xp_ref[...], nw) * (j != 0).astype(F32)
    xq = _rmsnorm(xq_ref[...], nw) * (j != blocks_per_seq - 1).astype(F32)
    xm_bf = xm.astype(BF16)
    xe_bf = jnp.concatenate([xp, xm, xq], axis=0).astype(BF16)

    slabs = col_chunk // LANES
    n_chunks = d_xbc // col_chunk

    def project(c):
        u = _dot(xe_bf, w_ref[:, d_z + c * col_chunk:d_z + (c + 1) * col_chunk])
        for s in range(slabs):
            p_ref[c % 2, s] = u[:, s * LANES:(s + 1) * LANES]

    def conv(c):
        span = tm // CONV_PHASES // 2
        for s in range(slabs):
            lanes = slice(c * col_chunk + s * LANES, c * col_chunk + (s + 1) * LANES)
            for rb in range(tm // CONV_PHASES // span):
                base = rb * span * CONV_PHASES
                acc = [cb_ref[:, lanes]] * CONV_PHASES
                for o in range(CONV_PHASES + SSD_CONV - 1):
                    rows = p_ref[c % 2, s, pl.ds(base + o + HALO - SSD_CONV // 2, span, stride=CONV_PHASES), :]
                    for ph in range(CONV_PHASES):
                        if 0 <= o - ph < SSD_CONV:
                            acc[ph] = acc[ph] + cw_ref[o - ph:o - ph + 1, lanes] * rows
                for ph in range(CONV_PHASES):
                    a_ref[s, pl.ds(base + ph, span, stride=CONV_PHASES), :] = _silu(acc[ph])
            xbc_ref[:, lanes] = a_ref[s].astype(BF16)

    def project_z(c):
        sl = slice(c * col_chunk, (c + 1) * col_chunk)
        z_ref[:, sl] = _dot(xm_bf, w_ref[:, sl]).astype(BF16)

    z_chunks = list(range(d_z // col_chunk))
    project(0)
    for c in range(n_chunks):
        if c + 1 < n_chunks:
            project(c + 1)
        if c % 2 == 1 and z_chunks:
            project_z(z_chunks.pop(0))
        conv(c)
    dt_raw = _dot(xm_bf, w_ref[:, d_z + d_xbc:])
    dt_raw = jnp.concatenate([dt_raw, jnp.zeros((tm, LANES - dt_raw.shape[1]), F32)], axis=1)
    dt_ref[...] = jax.nn.softplus(dt_raw + dtb_ref[...])
    for c in z_chunks:
        project_z(c)


def _ssd_in(x, nw, w_in, cw, cb, dtb, *, seq, d_z):
    n, d = x.shape
    d_xbc = cw.shape[1]
    tm = min(ROW_TILE, seq)
    bps = seq // tm
    hb = tm // HALO
    last_halo = n // HALO - 1
    row = lambda w: pl.BlockSpec((tm, w), lambda i: (i, 0))
    prev = pl.BlockSpec((HALO, d), lambda i: (jnp.maximum(i * hb - 1, 0), 0))
    nxt = pl.BlockSpec((HALO, d), lambda i: (jnp.minimum((i + 1) * hb, last_halo), 0))
    col_chunk = 2 * MXU_WIDTH
    return pl.pallas_call(
        functools.partial(_ssd_in_kernel, blocks_per_seq=bps, col_chunk=col_chunk),
        grid=(n // tm,),
        in_specs=[prev, row(d), nxt, _resident((1, d)), _resident(w_in.shape),
                  _resident(cw.shape), _resident(cb.shape), _resident(dtb.shape)],
        out_specs=[row(d_z), row(d_xbc), row(LANES)],
        out_shape=[jax.ShapeDtypeStruct((n, d_z), BF16),
                   jax.ShapeDtypeStruct((n, d_xbc), BF16),
                   jax.ShapeDtypeStruct((n, LANES), F32)],
        scratch_shapes=[pltpu.VMEM((2, col_chunk // LANES, tm + 2 * HALO, LANES), F32),
                        pltpu.VMEM((col_chunk // LANES, tm, LANES), F32)],
        compiler_params=_params("arbitrary"),
        name="ssd_in",
    )(x, x, x, nw, w_in, cw, cb, dtb)


def _chunk_decays(dt, a_row, tril, n_heads):
    t = dt.shape[0]
    da = dt * a_row
    prefix = jnp.dot(tril, da, precision=lax.Precision.HIGHEST, preferred_element_type=F32)
    total = prefix[t - 1:t, :]
    lane = lax.broadcasted_iota(jnp.int32, dt.shape, 1)
    return jnp.where(lane < n_heads, prefix, total - prefix + da), total


def _bcol(v_t, j):
    return jnp.broadcast_to(v_t[j:j + 1, :], (LANES, v_t.shape[1])).T


def _head_lanes(cols):
    first = lax.broadcasted_iota(jnp.int32, cols[0].shape, 1) < SSD_HEAD_DIM
    return jnp.concatenate([jnp.where(first, cols[0], cols[1]), jnp.where(first, cols[2], cols[3])], axis=1)


def _block_diag(xg):
    lane = lax.broadcasted_iota(jnp.int32, xg.shape, 1)
    zero = jnp.zeros_like(xg)
    return jnp.concatenate(
        [jnp.where((lane >= e * SSD_HEAD_DIM) & (lane < (e + 1) * SSD_HEAD_DIM), xg, zero)
         for e in range(SSD_GROUP_HEADS)], axis=0)


def _ssd_bwd_kernel(xbc_ref, dt_ref, alog_ref, yb_ref, st_ref, *, n_heads, n_groups):
    t = SSD_CHUNK
    gh = SSD_GROUP_HEADS
    gw = gh * SSD_HEAD_DIM
    d_inner = n_heads * SSD_HEAD_DIM
    n_chunks = xbc_ref.shape[0] // t

    @pl.when(pl.program_id(1) == 0)
    def _():
        st_ref[...] = jnp.zeros_like(st_ref)

    a_row = -jnp.exp(alog_ref[...])
    ri = lax.broadcasted_iota(jnp.int32, (t, t), 0)
    ci = lax.broadcasted_iota(jnp.int32, (t, t), 1)
    tril = (ri >= ci).astype(F32)

    def chunk(i, carry):
        r0 = pl.multiple_of((n_chunks - 1 - i) * t, t)
        rows = pl.ds(r0, t)
        dt = dt_ref[rows, :]
        cum, total = _chunk_decays(dt, a_row, tril, n_heads)
        ws_t = (dt * jnp.exp(total - cum)).T
        decay = jnp.exp(total)
        for g in range(n_groups):
            h0 = n_heads + g * gh
            xg = xbc_ref[rows, g * gw:(g + 1) * gw]
            bg = xbc_ref[rows, d_inner + g * SSD_STATE:d_inner + (g + 1) * SSD_STATE]
            cg = xbc_ref[rows, d_inner + (n_groups + g) * SSD_STATE:d_inner + (n_groups + g + 1) * SSD_STATE]
            st = st_ref[g]
            yb_ref[rows, g * gw:(g + 1) * gw] = _dot(cg, st.astype(BF16)).astype(BF16)
            bt = bg.astype(F32).T
            bws = [(bt * ws_t[h0 + e:h0 + e + 1, :]).astype(BF16) for e in range(gh)]
            decay_g = _head_lanes([jnp.broadcast_to(decay[:, h0 + e:h0 + e + 1], (1, LANES)) for e in range(gh)])
            st_ref[g] = st * decay_g + _dot(jnp.concatenate(bws, axis=1), _block_diag(xg))
        return carry

    lax.fori_loop(0, n_chunks, chunk, 0, unroll=True)


def _ssd_main_kernel(xbc_ref, dt_ref, z_ref, yb_ref, x_ref, alog_ref, dexp_ref, nw_ref, wout_ref,
                     o_ref, st_ref, y_ref, yn_ref, *, n_heads, n_groups):
    t = SSD_CHUNK
    gh = SSD_GROUP_HEADS
    gw = gh * SSD_HEAD_DIM
    d_inner = n_heads * SSD_HEAD_DIM
    n_chunks = xbc_ref.shape[0] // t

    @pl.when(pl.program_id(1) == 0)
    def _():
        st_ref[...] = jnp.zeros_like(st_ref)

    a_row = -jnp.exp(alog_ref[...])
    ri = lax.broadcasted_iota(jnp.int32, (t, t), 0)
    ci = lax.broadcasted_iota(jnp.int32, (t, t), 1)
    lower = ri >= ci
    tril = lower.astype(F32)

    def chunk(i):
        rows = slice(i * t, (i + 1) * t)
        dt = dt_ref[rows, :]
        cum, total = _chunk_decays(dt, a_row, tril, n_heads)
        cum_t = (cum * LOG2E).T
        dt_t = dt.T
        ws_t = (dt * jnp.exp(total - cum)).T
        for g in range(n_groups):
            cols = slice(g * gw, (g + 1) * gw)
            xg = xbc_ref[rows, cols]
            bg = xbc_ref[rows, d_inner + g * SSD_STATE:d_inner + (g + 1) * SSD_STATE]
            cg = xbc_ref[rows, d_inner + (n_groups + g) * SSD_STATE:d_inner + (n_groups + g + 1) * SSD_STATE]
            bt = bg.astype(F32).T
            cb = _dot(cg, bt.astype(BF16)).astype(BF16)
            ms, bws, ef, eb = [], [], [], []
            for e in range(gh):
                hf = g * gh + e
                hb = n_heads + hf
                cf_col = _bcol(cum_t, hf)
                cb_col = _bcol(cum_t, hb)
                arg = jnp.where(lower, cf_col - cum_t[hf:hf + 1, :], cb_col - cum_t[hb:hb + 1, :])
                dtf = dt_t[hf:hf + 1, :]
                dtb = dt_t[hb:hb + 1, :]
                w = jnp.where(ri > ci, dtf, jnp.where(ri < ci, dtb, dtf + dtb))
                ms.append((jnp.exp2(arg) * w).astype(BF16) * cb)
                bws.append((bt * ws_t[hf:hf + 1, :]).astype(BF16))
                ef.append(cf_col)
                eb.append(cb_col)
            lhs = jnp.concatenate([jnp.concatenate(ms, axis=1), jnp.concatenate(bws, axis=1)], axis=0)
            r = _dot(lhs, _block_diag(xg))
            e_f = jnp.exp2(_head_lanes(ef))
            st = st_ref[g]
            y_ref[rows, cols] = (r[:t] + xg.astype(F32) * dexp_ref[:, cols] + _dot(cg, st.astype(BF16)) * e_f
                                 + yb_ref[rows, cols].astype(F32) * jnp.exp2(_head_lanes(eb)))
            st_ref[g] = st * e_f[t - 1:t, :] + r[t:]

    for i in range(n_chunks):
        chunk(i)

    half = y_ref.shape[0] // 2

    def gate_norm(r):
        rows = slice(r * half, (r + 1) * half)
        y = y_ref[rows, :] * _silu(z_ref[rows, :].astype(F32))
        yn_ref[r] = _rmsnorm(y, nw_ref[...]).astype(BF16)

    def project_out(r):
        rows = slice(r * half, (r + 1) * half)
        o_ref[rows, :] = x_ref[rows, :] + _dot(yn_ref[r], wout_ref[...])

    gate_norm(0)
    project_out(0)
    gate_norm(1)
    project_out(1)


def _ssd_scan(x, xbc, dt, z, alog, dexp, nw, wout, *, batch, seq, n_heads, n_groups):
    n, d = x.shape
    d_inner = n_heads * SSD_HEAD_DIM
    state = pltpu.VMEM((n_groups, SSD_STATE, SSD_GROUP_HEADS * SSD_HEAD_DIM), F32)
    tr = min(SSD_BWD_ROW_TILE, seq)
    rps = seq // tr
    rev = lambda w: pl.BlockSpec((tr, w), lambda b, j: (b * rps + rps - 1 - j, 0))

    yb = pl.pallas_call(
        functools.partial(_ssd_bwd_kernel, n_heads=n_heads, n_groups=n_groups),
        grid=(batch, rps),
        in_specs=[rev(xbc.shape[1]), rev(LANES), _resident(alog.shape)],
        out_specs=rev(d_inner),
        out_shape=jax.ShapeDtypeStruct((n, d_inner), BF16),
        scratch_shapes=[state],
        compiler_params=_params("arbitrary", "arbitrary"),
        name="ssd_bwd",
    )(xbc, dt, alog)

    tb = min(ROW_TILE, seq)
    bps = seq // tb
    fwd = lambda w: pl.BlockSpec((tb, w), lambda b, j: (b * bps + j, 0))
    return pl.pallas_call(
        functools.partial(_ssd_main_kernel, n_heads=n_heads, n_groups=n_groups),
        grid=(batch, bps),
        in_specs=[fwd(xbc.shape[1]), fwd(LANES), fwd(d_inner), fwd(d_inner), fwd(d),
                  _resident(alog.shape), _resident(dexp.shape), _resident(nw.shape), _resident(wout.shape)],
        out_specs=fwd(d),
        out_shape=jax.ShapeDtypeStruct((n, d), F32),
        scratch_shapes=[state, pltpu.VMEM((tb, d_inner), F32), pltpu.VMEM((2, tb // 2, d_inner), BF16)],
        compiler_params=_params("arbitrary", "arbitrary"),
        name="ssd_main",
    )(xbc, dt, z, yb, x, alog, dexp, nw, wout)


def _attn_qkv_kernel(x_ref, nw_ref, wt_ref, qnt_ref, knt_ref, cost_ref, sint_ref, qt_ref, k_ref, vt_ref, y_ref):
    dh = ATTN_HEAD_DIM
    dq = qt_ref.shape[0]
    dk = vt_ref.shape[0]
    xn = _rmsnorm(x_ref[...], nw_ref[...]).astype(BF16)
    cost = cost_ref[...]
    sint = sint_ref[...]

    def norm_rope(a, w_col, scale):
        an = a * lax.rsqrt(jnp.sum(a * a, axis=0, keepdims=True) * (1.0 / dh) + EPS) * w_col
        x1, x2 = an[:dh // 2], an[dh // 2:]
        return jnp.concatenate([x1 * cost - x2 * sint, x2 * cost + x1 * sint], axis=0) * scale

    q_scale = dh ** -0.5 * LOG2E
    qnt = qnt_ref[...]
    knt = knt_ref[...]
    n_q = dq // MXU_WIDTH

    def project(c):
        rows = min(MXU_WIDTH, dq + dk - c * MXU_WIDTH)
        y_ref[c % 2, 0:rows, :] = _dot_nt(wt_ref[c * MXU_WIDTH:c * MXU_WIDTH + rows, :], xn)

    def finish_q(c):
        for hh in range(MXU_WIDTH // dh):
            qt_ref[c * MXU_WIDTH + hh * dh:c * MXU_WIDTH + (hh + 1) * dh, :] = norm_rope(
                y_ref[c % 2, hh * dh:(hh + 1) * dh, :], qnt, q_scale).astype(BF16)

    assert dq % MXU_WIDTH == 0 and dk <= MXU_WIDTH
    project(0)
    for c in range(n_q):
        project(c + 1)
        finish_q(c)
    vt_ref[...] = _dot_nt(wt_ref[dq + dk:, :], xn).astype(BF16)
    kt = jnp.concatenate(
        [norm_rope(y_ref[n_q % 2, g * dh:(g + 1) * dh, :], knt, 1.0) for g in range(dk // dh)], axis=0)
    k_ref[...] = kt.T.astype(BF16)


def _attn_core_kernel(sink_ref, qt_ref, kp_ref, k_ref, kq_ref, vtp_ref, vt_ref, vtq_ref, x_ref, woutt_ref,
                      o_ref, kbuf_ref, vbuf_ref, obuf_ref, s_ref, p_ref, inv_ref, *, blocks_per_seq, n_kv, group):
    blk = ATTN_BLOCK
    dh = ATTN_HEAD_DIM
    tq = qt_ref.shape[1]
    n_sub = tq // blk
    j = lax.rem(pl.program_id(0), blocks_per_seq)
    kbuf_ref[0:blk, :] = kp_ref[...]
    kbuf_ref[blk:blk + tq, :] = k_ref[...]
    kbuf_ref[blk + tq:, :] = kq_ref[...]
    vbuf_ref[:, 0:blk] = vtp_ref[...]
    vbuf_ref[:, blk:blk + tq] = vt_ref[...]
    vbuf_ref[:, blk + tq:] = vtq_ref[...]

    si = lax.broadcasted_iota(jnp.int32, (3 * blk, blk), 0)
    qi = lax.broadcasted_iota(jnp.int32, (3 * blk, blk), 1)
    rel = si - blk - qi
    band = (rel <= blk) & (rel >= -blk)

    def bias_of(i):
        valid = band
        if i == 0:
            valid = valid & jnp.logical_not((j == 0) & (si < blk))
        if i == n_sub - 1:
            valid = valid & jnp.logical_not((j == blocks_per_seq - 1) & (si >= 2 * blk))
        return jnp.tile(jnp.where(valid, 0.0, -jnp.inf), (1, group))

    def scores(i, g, slot):
        r0 = i * blk
        kk = kbuf_ref[r0:r0 + 3 * blk, (g // 2) * LANES:(g // 2 + 1) * LANES]
        qcat = jnp.concatenate(
            [qt_ref[(g * group + e) * dh:(g * group + e + 1) * dh, r0:r0 + blk] for e in range(group)], axis=1)
        zero = jnp.zeros_like(qcat)
        s_ref[slot] = _dot(kk, jnp.concatenate([qcat, zero] if g % 2 == 0 else [zero, qcat], axis=0))

    def softmax(bias, g, slot):
        s = s_ref[slot] + bias
        sink = jnp.concatenate(
            [jnp.full((1, blk), sink_ref[g * group + e] * LOG2E, F32) for e in range(group)], axis=1)
        m = jnp.maximum(jnp.max(s, axis=0, keepdims=True), sink)
        pr = jnp.exp2(s - m)
        inv_ref[slot] = 1.0 / (jnp.sum(pr, axis=0, keepdims=True) + jnp.exp2(sink - m))
        p_ref[slot] = pr.astype(BF16)

    def values(i, g, slot):
        r0 = i * blk
        ot = _dot(vbuf_ref[g * dh:(g + 1) * dh, r0:r0 + 3 * blk], p_ref[slot]) * inv_ref[slot]
        for e in range(group):
            obuf_ref[(g * group + e) * dh:(g * group + e + 1) * dh, r0:r0 + blk] = (
                ot[:, e * blk:(e + 1) * blk].astype(BF16))

    def project_out(half):
        rows = slice(half * tq // 2, (half + 1) * tq // 2)
        o_ref[rows, :] = x_ref[rows, :] + _dot(woutt_ref[...], obuf_ref[:, rows]).T

    items = [(i, g) for i in range(n_sub) for g in range(n_kv)]
    biases = [bias_of(i) for i in range(n_sub)]
    scores(*items[0], 0)
    for n, (i, g) in enumerate(items):
        if n + 1 < len(items):
            scores(*items[n + 1], (n + 1) % 2)
        softmax(biases[i], g, n % 2)
        if n > 0:
            values(*items[n - 1], (n - 1) % 2)
        if n == len(items) // 2:
            project_out(0)
    values(*items[-1], (len(items) - 1) % 2)
    project_out(1)


def _attention(x, nw, wt, qnt, knt, tables, sink, woutt, *, seq, n_kv):
    n, d = x.shape
    n_heads = sink.shape[0]
    dq = n_heads * ATTN_HEAD_DIM
    dk = n_kv * ATTN_HEAD_DIM
    assert n_kv % 2 == 0 and wt.shape[0] == dq + 2 * dk
    cost, sint = tables
    row = lambda t, w: pl.BlockSpec((t, w), lambda i: (i, 0))
    col = lambda t, h: pl.BlockSpec((h, t), lambda i: (0, i))

    tm = min(ATTN_QKV_ROW_TILE, seq)
    post = pl.BlockSpec((ATTN_HEAD_DIM // 2, tm), lambda i: (0, lax.rem(i, seq // tm)))
    qt, k, vt = pl.pallas_call(
        _attn_qkv_kernel,
        grid=(n // tm,),
        in_specs=[row(tm, d), _resident((1, d)), _resident(wt.shape), _resident(qnt.shape),
                  _resident(knt.shape), post, post],
        out_specs=[col(tm, dq), row(tm, dk), col(tm, dk)],
        out_shape=[jax.ShapeDtypeStruct((dq, n), BF16), jax.ShapeDtypeStruct((n, dk), BF16),
                   jax.ShapeDtypeStruct((dk, n), BF16)],
        scratch_shapes=[pltpu.VMEM((2, MXU_WIDTH, tm), F32)],
        compiler_params=_params("arbitrary"),
        name="attn_qkv",
    )(x, nw, wt, qnt, knt, cost, sint)

    tc = min(ROW_TILE, seq)
    sb = tc // ATTN_BLOCK
    last = n // ATTN_BLOCK - 1
    before = lambda i: jnp.maximum(i * sb - 1, 0)
    after = lambda i: jnp.minimum((i + 1) * sb, last)
    return pl.pallas_call(
        functools.partial(_attn_core_kernel, blocks_per_seq=seq // tc, n_kv=n_kv, group=n_heads // n_kv),
        grid=(n // tc,),
        in_specs=[pl.BlockSpec(memory_space=pltpu.SMEM), col(tc, dq),
                  pl.BlockSpec((ATTN_BLOCK, dk), lambda i: (before(i), 0)), row(tc, dk),
                  pl.BlockSpec((ATTN_BLOCK, dk), lambda i: (after(i), 0)),
                  pl.BlockSpec((dk, ATTN_BLOCK), lambda i: (0, before(i))), col(tc, dk),
                  pl.BlockSpec((dk, ATTN_BLOCK), lambda i: (0, after(i))),
                  row(tc, d), _resident(woutt.shape)],
        out_specs=row(tc, d),
        out_shape=jax.ShapeDtypeStruct((n, d), F32),
        scratch_shapes=[pltpu.VMEM((tc + 2 * ATTN_BLOCK, dk), BF16), pltpu.VMEM((dk, tc + 2 * ATTN_BLOCK), BF16),
                        pltpu.VMEM((dq, tc), BF16),
                        pltpu.VMEM((2, 3 * ATTN_BLOCK, n_heads // n_kv * ATTN_BLOCK), F32),
                        pltpu.VMEM((2, 3 * ATTN_BLOCK, n_heads // n_kv * ATTN_BLOCK), BF16),
                        pltpu.VMEM((2, 1, n_heads // n_kv * ATTN_BLOCK), F32)],
        compiler_params=_params("arbitrary"),
        name="attn_core",
    )(sink, qt, k, k, k, vt, vt, vt, x, woutt)


def _rope_tables(seq):
    half = ATTN_HEAD_DIM // 2
    inv_freq = ROPE_THETA ** (-jnp.arange(half, dtype=F32) / half)
    ang = inv_freq[:, None] * jnp.arange(seq, dtype=jnp.int32).astype(F32)[None, :]
    return jnp.cos(ang), jnp.sin(ang)


def kernel(x, norm_w, ffn_w_gate, ffn_w_up, ffn_w_down, ssd_w_in, ssd_conv_w, ssd_conv_b, ssd_dt_bias,
           ssd_a_log, ssd_d, ssd_norm_w, ssd_w_out, attn_w_qkv, attn_q_norm, attn_k_norm, attn_sink,
           attn_w_out):
    batch, seq, d = x.shape
    depth = norm_w.shape[0]
    n_ssd_heads = ssd_d.shape[1]
    d_inner = n_ssd_heads * SSD_HEAD_DIM
    conv_ch = ssd_conv_w.shape[2]
    n_groups = (conv_ch - d_inner) // (2 * SSD_STATE)
    n_kv = (attn_w_qkv.shape[2] // ATTN_HEAD_DIM - attn_sink.shape[1]) // 2
    bf = lambda w: w.astype(BF16)
    pad_lanes = lambda v: jnp.pad(v.astype(F32).reshape(1, -1), ((0, 0), (0, LANES - v.size)))
    tables = _rope_tables(seq)

    wg, wu, wd = bf(ffn_w_gate), bf(ffn_w_up), bf(ffn_w_down)
    h = x.reshape(batch * seq, d)
    for i in range(depth):
        h = _ffn(h, norm_w[i, 0][None], wg, wu, wd, i, 0)
        nw = norm_w[i, 1][None]
        j = i // 2
        if i % 2 == 0:
            z, xbc, dt = _ssd_in(h, nw, bf(ssd_w_in[j]), ssd_conv_w[j], ssd_conv_b[j][None],
                                 pad_lanes(ssd_dt_bias[j]), seq=seq, d_z=d_inner)
            h = _ssd_scan(h, xbc, dt, z, pad_lanes(ssd_a_log[j]),
                          jnp.repeat(ssd_d[j], SSD_HEAD_DIM)[None], ssd_norm_w[j][None], bf(ssd_w_out[j]),
                          batch=batch, seq=seq, n_heads=n_ssd_heads, n_groups=n_groups)
        else:
            h = _attention(h, nw, bf(attn_w_qkv[j].T), attn_q_norm[j][:, None], attn_k_norm[j][:, None], tables,
                           attn_sink[j], bf(attn_w_out[j].T), seq=seq, n_kv=n_kv)
        h = _ffn(h, norm_w[i, 2][None], wg, wu, wd, i, 1)
    return h.reshape(batch, seq, d)
```
